```python
import math
import jax
import jax.numpy as jnp
from jax import lax
import numpy as np

D_MODEL = 1024
BATCH = 8
SEQ = 2048
DEPTH = 2
DEC_BATCH = 128
DEC_SEQ = 8
PAST_LEN = 2048
PAGE_SIZE = 128

HEAD_DIM = 64
FOX_HEADS = 8
FOX_WIDTH = FOX_HEADS * HEAD_DIM
FOX_Q_BLOCK = 128
MOBA_HEADS = 8
MOBA_WIDTH = MOBA_HEADS * HEAD_DIM
MOBA_BLOCK = 256
MOBA_TOPK = 3
MOBA_Q_CHUNK = 32
T5_BUCKETS = 32
T5_MAX_DIST = 128
SSM_INNER = D_MODEL
SSM_HEAD_DIM = 64
SSM_HEADS = SSM_INNER // SSM_HEAD_DIM
SSM_GROUPS = 2
SSM_STATE = 128
SSM_CONV = 4
SSM_CHUNK = 128
CONV_DIM = SSM_INNER + 2 * SSM_GROUPS * SSM_STATE
N_BRANCH = 3
EXPERT_GROUPS = 4
EXPERTS_PER_GROUP = 8
N_EXPERTS = EXPERT_GROUPS * EXPERTS_PER_GROUP
TOP_K_INNER = 2
D_EXPERT = D_MODEL // 2
MOE_BLOCK = 128
RMS_EPS = 1e-6
SPLIT_SIZES = (FOX_WIDTH, FOX_WIDTH, FOX_WIDTH, FOX_HEADS, MOBA_WIDTH, MOBA_WIDTH, MOBA_WIDTH, SSM_INNER, CONV_DIM, SSM_HEADS, N_BRANCH * D_MODEL)
D_IN = sum(SPLIT_SIZES)

kernel_name = 'hybrid_fox_moba_ssd_hmoe_step'


def rmsnorm(x, gain):
    xf = x.astype(jnp.float32)
    y = xf * lax.rsqrt(jnp.mean(xf * xf, axis=-1, keepdims=True) + RMS_EPS)
    return (y * gain.astype(jnp.float32)).astype(x.dtype)


def t5_bucket(dist):
    max_exact = T5_BUCKETS // 2
    d = jnp.maximum(dist, 0)
    ratio = jnp.maximum(d, 1).astype(jnp.float32) / max_exact
    large = max_exact + (jnp.log(ratio) / math.log(T5_MAX_DIST / max_exact) * (T5_BUCKETS - max_exact)).astype(jnp.int32)
    return jnp.where(d < max_exact, d, jnp.minimum(large, T5_BUCKETS - 1))


def gather_pages(pool, page_table):
    g = pool[page_table]
    return g.reshape((g.shape[0], g.shape[1] * g.shape[2]) + g.shape[3:])


def fox_attention(q, k, v, cum_logf, q_offset):
    B, Q, H, dh = q.shape
    L = k.shape[1]
    qb = math.gcd(Q, FOX_Q_BLOCK)
    nqb = Q // qb
    scale = dh ** -0.5
    c_key = jnp.transpose(cum_logf, (0, 2, 1))
    c_qry = c_key[:, :, q_offset:q_offset + Q]
    key_pos = jnp.arange(L)
    q_blocks = q.reshape(B, nqb, qb, H, dh).transpose(1, 0, 2, 3, 4)
    c_blocks = c_qry.reshape(B, H, nqb, qb).transpose(2, 0, 1, 3)
    pos_blocks = (q_offset + jnp.arange(Q)).reshape(nqb, qb)

    def one_block(args):
        qi, ci, pi = args
        s = jnp.einsum('bqhd,bkhd->bhqk', qi, k, preferred_element_type=jnp.float32) * scale
        s = s + ci[..., None] - c_key[:, :, None, :]
        s = jnp.where(key_pos[None, None, None, :] <= pi[None, None, :, None], s, -jnp.inf)
        p = jax.nn.softmax(s, axis=-1)
        return jnp.einsum('bhqk,bkhd->bqhd', p.astype(v.dtype), v)

    out = lax.map(one_block, (q_blocks, c_blocks, pos_blocks))
    return out.transpose(1, 0, 2, 3, 4).reshape(B, Q, H, dh)


def moba_attention(q, k, v, rel_bias, q_offset):
    B, Q, H, dh = q.shape
    L = k.shape[1]
    nblk = -(-L // MOBA_BLOCK)
    pad = nblk * MOBA_BLOCK - L
    kb = jnp.pad(k, ((0, 0), (0, pad), (0, 0), (0, 0))).reshape(B, nblk, MOBA_BLOCK, H, dh).transpose(0, 3, 1, 2, 4)
    vb = jnp.pad(v, ((0, 0), (0, pad), (0, 0), (0, 0))).reshape(B, nblk, MOBA_BLOCK, H, dh).transpose(0, 3, 1, 2, 4)
    kmean = jnp.mean(kb.astype(jnp.float32), axis=3)
    topk = min(MOBA_TOPK, nblk)
    qc = math.gcd(Q, MOBA_Q_CHUNK)
    nqc = Q // qc
    q_chunks = q.reshape(B, nqc, qc, H, dh).transpose(0, 1, 3, 2, 4)
    qpos_all = (q_offset + jnp.arange(Q)).reshape(nqc, qc)
    blk_ids = jnp.arange(nblk)
    in_blk = jnp.arange(MOBA_BLOCK)
    hidx = jnp.arange(H)[:, None, None]
    scale = dh ** -0.5

    def per_seq(args):
        q_seq, kb_s, vb_s, km_s = args

        def per_chunk(cargs):
            qi, qpos = cargs
            own = qpos // MOBA_BLOCK
            gate = jnp.einsum('hqd,hnd->hqn', qi.astype(jnp.float32), km_s)
            gate = jnp.where(blk_ids[None, None, :] < own[None, :, None], gate, -jnp.inf)
            _, sel = lax.top_k(gate, topk)
            own_b = jnp.broadcast_to(own[None, :, None], (H, qc, 1))
            idx = jnp.concatenate([sel, own_b], axis=-1)
            valid = jnp.concatenate([sel < own[None, :, None], jnp.ones((H, qc, 1), bool)], axis=-1)
            kg = kb_s[hidx, idx]
            vg = vb_s[hidx, idx]
            kpos = idx[..., None] * MOBA_BLOCK + in_blk
            s = jnp.einsum('hqd,hqjkd->hqjk', qi, kg, preferred_element_type=jnp.float32) * scale
            s = s + rel_bias[t5_bucket(qpos[None, :, None, None] - kpos), hidx[..., None]].astype(jnp.float32)
            mask = valid[..., None] & (kpos <= qpos[None, :, None, None])
            s = jnp.where(mask, s, -jnp.inf)
            nsel = s.shape[2]
            p = jax.nn.softmax(s.reshape(H, qc, nsel * MOBA_BLOCK), axis=-1).reshape(H, qc, nsel, MOBA_BLOCK)
            return jnp.einsum('hqjk,hqjkd->hqd', p.astype(vg.dtype), vg)

        return lax.map(per_chunk, (q_seq, qpos_all))

    out = lax.map(per_seq, (q_chunks, kb, vb, kmean))
    return out.transpose(0, 1, 3, 2, 4).reshape(B, Q, H, dh)


def ssd_scan(xh, dt, A, Bg, Cg, h0):
    Bsz, L, G, R, P = xh.shape
    cl = math.gcd(L, SSM_CHUNK)
    nc = L // cl
    a = (dt * A).reshape(Bsz, nc, cl, G, R)
    xdt = (xh * dt[..., None]).reshape(Bsz, nc, cl, G, R, P)
    Bc = Bg.reshape(Bsz, nc, cl, G, -1)
    Cc = Cg.reshape(Bsz, nc, cl, G, -1)
    acum = jnp.cumsum(a, axis=2)
    seg = acum[:, :, :, None] - acum[:, :, None]
    causal = jnp.tril(jnp.ones((cl, cl), bool))[None, None, :, :, None, None]
    decay = jnp.exp(jnp.where(causal, seg, -jnp.inf))
    cb = jnp.einsum('bclgn,bcsgn->bclsg', Cc, Bc)
    y_diag = jnp.einsum('bclsgr,bcsgrp->bclgrp', cb[..., None] * decay, xdt)
    to_end = jnp.exp(acum[:, :, -1:] - acum)
    chunk_states = jnp.einsum('bclgn,bclgrp->bcgrpn', Bc, xdt * to_end[..., None])
    chunk_decay = jnp.exp(acum[:, :, -1])

    def step(h, inp):
        st, dec = inp
        return h * dec[..., None, None] + st, h

    h_final, h_prev = lax.scan(step, h0, (jnp.moveaxis(chunk_states, 1, 0), jnp.moveaxis(chunk_decay, 1, 0)))
    h_prev = jnp.moveaxis(h_prev, 0, 1)
    y_off = jnp.einsum('bclgn,bcgrpn->bclgrp', Cc, h_prev) * jnp.exp(acum)[..., None]
    return (y_diag + y_off).reshape(Bsz, L, G, R, P), h_final


def ssd_mixer(z, xbc, dt_raw, conv_state, ssm_state, conv_w, conv_b, dt_bias, a_log, d_skip, norm_g):
    Bsz, L, _ = xbc.shape
    G, R, P, N = SSM_GROUPS, SSM_HEADS // SSM_GROUPS, SSM_HEAD_DIM, SSM_STATE
    f32 = jnp.float32
    xpad = jnp.concatenate([conv_state.astype(xbc.dtype), xbc], axis=1)
    new_conv = xpad[:, xpad.shape[1] - (SSM_CONV - 1):]
    conv = lax.conv_general_dilated(xpad, conv_w[:, None, :].astype(xbc.dtype), window_strides=(1,), padding='VALID',
                                    dimension_numbers=('NWC', 'WIO', 'NWC'), feature_group_count=CONV_DIM)
    u = jax.nn.silu(conv.astype(f32) + conv_b.astype(f32))
    xs = u[..., :SSM_INNER].reshape(Bsz, L, G, R, P)
    Bm = u[..., SSM_INNER:SSM_INNER + G * N].reshape(Bsz, L, G, N)
    Cm = u[..., SSM_INNER + G * N:].reshape(Bsz, L, G, N)
    dt = jax.nn.softplus(dt_raw.astype(f32) + dt_bias.astype(f32)).reshape(Bsz, L, G, R)
    A = -jnp.exp(a_log.astype(f32)).reshape(G, R)
    h0 = ssm_state.astype(f32).reshape(Bsz, G, R, P, N)
    y, h_final = ssd_scan(xs, dt, A, Bm, Cm, h0)
    y = y + xs * d_skip.astype(f32).reshape(G, R)[:, :, None]
    yg = (y * jax.nn.silu(z.astype(f32)).reshape(Bsz, L, G, R, P)).reshape(Bsz, L, G, R * P)
    yg = yg * lax.rsqrt(jnp.mean(yg * yg, axis=-1, keepdims=True) + RMS_EPS)
    out = (yg.reshape(Bsz, L, SSM_INNER) * norm_g.astype(f32)).astype(z.dtype)
    return out, new_conv, h_final.reshape(Bsz, SSM_HEADS, P, N).astype(ssm_state.dtype)


def mixing_sublayer(x, lp, rel_bias, past_fk, past_fv, past_flogf, past_mk, past_mv, conv_state, ssm_state):
    B, L, _ = x.shape
    p0 = past_fk.shape[1]
    h = rmsnorm(x, lp['g_mix'])
    proj = jnp.einsum('bld,de->ble', h, lp['w_in'])
    points = [int(v) for v in np.cumsum(SPLIT_SIZES)[:-1]]
    fq, fk, fv, ff, mq, mk, mv, z, xbc, dtr, gl = jnp.split(proj, points, axis=-1)
    fq = rmsnorm(fq.reshape(B, L, FOX_HEADS, HEAD_DIM), lp['fox_q_gain'])
    fk = rmsnorm(fk.reshape(B, L, FOX_HEADS, HEAD_DIM), lp['fox_k_gain'])
    fv = fv.reshape(B, L, FOX_HEADS, HEAD_DIM)
    logf = jax.nn.log_sigmoid(ff.astype(jnp.float32) + lp['fox_f_bias'].astype(jnp.float32))
    cum = jnp.cumsum(jnp.concatenate([past_flogf.astype(jnp.float32), logf], axis=1), axis=1)
    ya = fox_attention(fq, jnp.concatenate([past_fk.astype(fk.dtype), fk], axis=1),
                       jnp.concatenate([past_fv.astype(fv.dtype), fv], axis=1), cum, p0)
    mq = rmsnorm(mq.reshape(B, L, MOBA_HEADS, HEAD_DIM), lp['moba_q_gain'])
    mk = rmsnorm(mk.reshape(B, L, MOBA_HEADS, HEAD_DIM), lp['moba_k_gain'])
    mv = mv.reshape(B, L, MOBA_HEADS, HEAD_DIM)
    ym = moba_attention(mq, jnp.concatenate([past_mk.astype(mk.dtype), mk], axis=1),
                        jnp.concatenate([past_mv.astype(mv.dtype), mv], axis=1), rel_bias, p0)
    yc, new_conv, new_ssm = ssd_mixer(z, xbc, dtr, conv_state, ssm_state, lp['conv_w'], lp['conv_b'],
                                      lp['dt_bias'], lp['a_log'], lp['d_skip'], lp['ssm_norm_g'])
    gates = jax.nn.sigmoid(gl.astype(jnp.float32)).astype(x.dtype).reshape(B, L, N_BRANCH, D_MODEL)
    merged = (gates[:, :, 0] * (ya.reshape(B, L, FOX_WIDTH) @ lp['w_out_fox'])
              + gates[:, :, 1] * (ym.reshape(B, L, MOBA_WIDTH) @ lp['w_out_moba'])
              + gates[:, :, 2] * (yc @ lp['w_out_ssm']))
    return x + merged @ lp['w_o'], (fk, fv, logf, mk, mv, new_conv, new_ssm)


def routed_experts(h, e_id, e_w, w1, w3, w2):
    T, D = h.shape
    A = e_id.shape[0]
    tok = jnp.arange(A) // TOP_K_INNER
    order = jnp.argsort(e_id)
    se, stok, sw = e_id[order], tok[order], e_w[order]
    counts = jnp.zeros((N_EXPERTS,), jnp.int32).at[e_id].add(1)
    starts = jnp.cumsum(counts) - counts
    padded = (counts + MOE_BLOCK - 1) // MOE_BLOCK * MOE_BLOCK
    pend = jnp.cumsum(padded)
    pstarts = pend - padded
    dest = pstarts[se] + (jnp.arange(A) - starts[se])
    nblocks = -(-A // MOE_BLOCK) + N_EXPERTS
    cap = nblocks * MOE_BLOCK
    buf_tok = jnp.full((cap,), T, jnp.int32).at[dest].set(stok)
    buf_w = jnp.zeros((cap,), jnp.float32).at[dest].set(sw)
    blk_expert = jnp.minimum(jnp.searchsorted(pend, jnp.arange(nblocks) * MOE_BLOCK, side='right'), N_EXPERTS - 1)
    h_pad = jnp.concatenate([h, jnp.zeros((1, D), h.dtype)], axis=0)

    def run_block(args):
        toks, e = args
        xb = h_pad[toks]
        return (jax.nn.silu(xb @ w1[e]) * (xb @ w3[e])) @ w2[e]

    yb = lax.map(run_block, (buf_tok.reshape(nblocks, MOE_BLOCK), blk_expert))
    y = jnp.zeros((T + 1, D), h.dtype).at[buf_tok].add(yb.reshape(cap, D) * buf_w[:, None].astype(h.dtype))
    return y[:T]


def moe_sublayer(x, lp):
    B, L, D = x.shape
    T = B * L
    h = rmsnorm(x, lp['g_ffn']).reshape(T, D)
    hf = h.astype(jnp.float32)
    g_prob = jax.nn.softmax(hf @ lp['router_group_w'].astype(jnp.float32) + lp['router_group_b'].astype(jnp.float32), axis=-1)
    g_top, g_idx = lax.top_k(g_prob, 1)
    e_logits = (hf @ lp['router_expert_w'].astype(jnp.float32) + lp['router_expert_b'].astype(jnp.float32)).reshape(T, EXPERT_GROUPS, EXPERTS_PER_GROUP)
    e_in = e_logits[jnp.arange(T), g_idx[:, 0]]
    e_top, e_loc = lax.top_k(jax.nn.softmax(e_in, axis=-1), TOP_K_INNER)
    e_w = g_top * e_top / jnp.sum(e_top, axis=-1, keepdims=True)
    e_id = g_idx * EXPERTS_PER_GROUP + e_loc
    y = routed_experts(h, e_id.reshape(-1), e_w.reshape(-1), lp['expert_w1'], lp['expert_w3'], lp['expert_w2'])
    return x + y.reshape(B, L, D)


def setup_inputs(seed: int = 0) -> dict:
    key = jax.random.key(seed)
    ks = jax.random.split(key, 40)
    f32 = jnp.float32
    n_pages = PAST_LEN // PAGE_SIZE
    n_used = DEC_BATCH * n_pages
    n_pool = n_used + max(1, n_used // 4)

    def nrm(i, shape, scale):
        return jax.random.normal(ks[i], shape, f32) * scale

    dt0 = jnp.exp(jax.random.uniform(ks[20], (DEPTH, SSM_HEADS), f32) * (math.log(0.1) - math.log(0.001)) + math.log(0.001))
    return {
        'x_prompt': nrm(0, (BATCH, SEQ, D_MODEL), 1.0),
        'x_sample': nrm(1, (DEC_BATCH, DEC_SEQ, D_MODEL), 1.0),
        'cache_fox_k': nrm(2, (DEPTH, n_pool, PAGE_SIZE, FOX_HEADS, HEAD_DIM), 1.0),
        'cache_fox_v': nrm(3, (DEPTH, n_pool, PAGE_SIZE, FOX_HEADS, HEAD_DIM), 1.0),
        'cache_fox_logf': jax.nn.log_sigmoid(2.0 + nrm(4, (DEPTH, n_pool, PAGE_SIZE, FOX_HEADS), 1.0)),
        'cache_moba_k': nrm(5, (DEPTH, n_pool, PAGE_SIZE, MOBA_HEADS, HEAD_DIM), 1.0),
        'cache_moba_v': nrm(6, (DEPTH, n_pool, PAGE_SIZE, MOBA_HEADS, HEAD_DIM), 1.0),
        'state_conv': nrm(7, (DEPTH, DEC_BATCH, SSM_CONV - 1, CONV_DIM), 1.0),
        'state_ssm': nrm(8, (DEPTH, DEC_BATCH, SSM_HEADS, SSM_HEAD_DIM, SSM_STATE), 0.1),
        'page_table': jax.random.permutation(ks[9], n_pool)[:n_used].reshape(DEC_BATCH, n_pages).astype(jnp.int32),
        'rel_bias': nrm(10, (T5_BUCKETS, MOBA_HEADS), 0.5),
        'g_mix': 1.0 + nrm(11, (DEPTH, D_MODEL), 0.1),
        'w_in': nrm(12, (DEPTH, D_MODEL, D_IN), D_MODEL ** -0.5),
        'fox_q_gain': 1.0 + nrm(13, (DEPTH, HEAD_DIM), 0.1),
        'fox_k_gain': 1.0 + nrm(14, (DEPTH, HEAD_DIM), 0.1),
        'fox_f_bias': 2.0 + nrm(15, (DEPTH, FOX_HEADS), 0.1),
        'moba_q_gain': 1.0 + nrm(16, (DEPTH, HEAD_DIM), 0.1),
        'moba_k_gain': 1.0 + nrm(17, (DEPTH, HEAD_DIM), 0.1),
        'conv_w': nrm(18, (DEPTH, SSM_CONV, CONV_DIM), SSM_CONV ** -0.5),
        'conv_b': nrm(19, (DEPTH, CONV_DIM), 0.01),
        'dt_bias': dt0 + jnp.log(-jnp.expm1(-dt0)),
        'a_log': jnp.log(jax.random.uniform(ks[21], (DEPTH, SSM_HEADS), f32, 1.0, 16.0)),
        'd_skip': 1.0 + nrm(22, (DEPTH, SSM_HEADS), 0.1),
        'ssm_norm_g': 1.0 + nrm(23, (DEPTH, SSM_INNER), 0.1),
        'w_out_fox': nrm(24, (DEPTH, FOX_WIDTH, D_MODEL), FOX_WIDTH ** -0.5),
        'w_out_moba': nrm(25, (DEPTH, MOBA_WIDTH, D_MODEL), MOBA_WIDTH ** -0.5),
        'w_out_ssm': nrm(26, (DEPTH, SSM_INNER, D_MODEL), SSM_INNER ** -0.5),
        'w_o': nrm(27, (DEPTH, D_MODEL, D_MODEL), D_MODEL ** -0.5),
        'g_ffn': 1.0 + nrm(28, (DEPTH, D_MODEL), 0.1),
        'router_group_w': nrm(29, (DEPTH, D_MODEL, EXPERT_GROUPS), D_MODEL ** -0.5),
        'router_group_b': nrm(30, (DEPTH, EXPERT_GROUPS), 0.01),
        'router_expert_w': nrm(31, (DEPTH, D_MODEL, N_EXPERTS), D_MODEL ** -0.5),
        'router_expert_b': nrm(32, (DEPTH, N_EXPERTS), 0.01),
        'expert_w1': nrm(33, (DEPTH, N_EXPERTS, D_MODEL, D_EXPERT), D_MODEL ** -0.5),
        'expert_w3': nrm(34, (DEPTH, N_EXPERTS, D_MODEL, D_EXPERT), D_MODEL ** -0.5),
        'expert_w2': nrm(35, (DEPTH, N_EXPERTS, D_EXPERT, D_MODEL), D_EXPERT ** -0.5),
    }


def reference(x_prompt, x_sample, cache_fox_k, cache_fox_v, cache_fox_logf, cache_moba_k, cache_moba_v,
              state_conv, state_ssm, page_table, rel_bias, g_mix, w_in, fox_q_gain, fox_k_gain, fox_f_bias,
              moba_q_gain, moba_k_gain, conv_w, conv_b, dt_bias, a_log, d_skip, ssm_norm_g, w_out_fox,
              w_out_moba, w_out_ssm, w_o, g_ffn, router_group_w, router_group_b, router_expert_w,
              router_expert_b, expert_w1, expert_w3, expert_w2):
    bp = x_prompt.shape[0]
    dtype = x_prompt.dtype
    yp, ys = x_prompt, x_sample
    new_p = [[] for _ in range(7)]
    new_s = [[] for _ in range(7)]
    for l in range(DEPTH):
        lp = dict(g_mix=g_mix[l], w_in=w_in[l], fox_q_gain=fox_q_gain[l], fox_k_gain=fox_k_gain[l],
                  fox_f_bias=fox_f_bias[l], moba_q_gain=moba_q_gain[l], moba_k_gain=moba_k_gain[l],
                  conv_w=conv_w[l], conv_b=conv_b[l], dt_bias=dt_bias[l], a_log=a_log[l], d_skip=d_skip[l],
                  ssm_norm_g=ssm_norm_g[l], w_out_fox=w_out_fox[l], w_out_moba=w_out_moba[l],
                  w_out_ssm=w_out_ssm[l], w_o=w_o[l], g_ffn=g_ffn[l], router_group_w=router_group_w[l],
                  router_group_b=router_group_b[l], router_expert_w=router_expert_w[l],
                  router_expert_b=router_expert_b[l], expert_w1=expert_w1[l], expert_w3=expert_w3[l],
                  expert_w2=expert_w2[l])
        empty_f = jnp.zeros((bp, 0, FOX_HEADS, HEAD_DIM), dtype)
        empty_m = jnp.zeros((bp, 0, MOBA_HEADS, HEAD_DIM), dtype)
        yp, st_p = mixing_sublayer(yp, lp, rel_bias, empty_f, empty_f, jnp.zeros((bp, 0, FOX_HEADS), jnp.float32),
                                   empty_m, empty_m, jnp.zeros((bp, SSM_CONV - 1, CONV_DIM), dtype),
                                   jnp.zeros((bp, SSM_HEADS, SSM_HEAD_DIM, SSM_STATE), jnp.float32))
        yp = moe_sublayer(yp, lp)
        ys, st_s = mixing_sublayer(ys, lp, rel_bias,
                                   gather_pages(cache_fox_k[l], page_table), gather_pages(cache_fox_v[l], page_table),
                                   gather_pages(cache_fox_logf[l], page_table),
                                   gather_pages(cache_moba_k[l], page_table), gather_pages(cache_moba_v[l], page_table),
                                   state_conv[l], state_ssm[l])
        ys = moe_sublayer(ys, lp)
        for i in range(7):
            new_p[i].append(st_p[i])
            new_s[i].append(st_s[i])
    p_fk, p_fv, p_fl, p_mk, p_mv, p_conv, p_ssm = [jnp.stack(a) for a in new_p]
    s_fk, s_fv, s_fl, s_mk, s_mv, s_conv, s_ssm = [jnp.stack(a) for a in new_s]
    return (yp, ys, p_fk, p_fv, p_fl, p_mk, p_mv, p_conv, p_ssm, s_fk, s_fv, s_fl, s_mk, s_mv, s_conv, s_ssm)
```

```python
import functools
import math

import jax
import jax.numpy as jnp
from jax import lax
from jax.experimental import pallas as pl
from jax.experimental.pallas import tpu as pltpu

f32, bf16, i32 = jnp.float32, jnp.bfloat16, jnp.int32
HI = lax.Precision.HIGHEST
NT = (((1,), (1,)), ((), ()))
TN = (((0,), (0,)), ((), ()))

D_MODEL = 1024
HEAD_DIM = 64
N_HEADS = 8
ATT_W = N_HEADS * HEAD_DIM
PAGE = 128
MOBA_BLOCK = 256
MOBA_TOPK = 3
T5_BUCKETS = 32
T5_MAX_DIST = 128
SSM_HEADS = 16
SSM_P = 64
SSM_N = 128
SSM_GROUPS = 2
SSM_INNER = 1024
SSM_CONV = 4
CONV_DIM = 1536
N_GROUPS_E = 4
E_PER_GROUP = 8
N_EXPERTS = 32
D_EXPERT = 512
RMS_EPS = 1e-6
VMEM_LIMIT = 56 * 1024 * 1024

C_FQ, C_FK, C_MQ, C_MK, C_FV, C_MV, C_Z, C_XS, C_GL, C_BC, C_END = (
    0, 512, 1024, 1536, 2048, 2560, 3072, 4096, 5120, 8192, 8704)
N_NORMED = 4
PROJ_TN = 512
S_DT, S_FF, S_W = 0, 16, 128


def _cp(sem):
    return pltpu.CompilerParams(dimension_semantics=sem, vmem_limit_bytes=VMEM_LIMIT)


def _softplus(x):
    return jnp.maximum(x, 0.0) + jnp.log1p(jnp.exp(-jnp.abs(x)))


def _silu(x):
    return x * (1.0 / (1.0 + jnp.exp(-x)))


def _bias_tiles_body(rb_ref, d_ref, p_ref):
    h = pl.program_id(0)
    n = MOBA_BLOCK
    row = lax.broadcasted_iota(i32, (n, n), 0)
    col = lax.broadcasted_iota(i32, (n, n), 1)
    max_exact = T5_BUCKETS // 2
    for ref, off in ((d_ref, 0), (p_ref, n)):
        d = jnp.maximum(row - col + off, 0)
        ratio = jnp.maximum(d, 1).astype(f32) / max_exact
        large = max_exact + (jnp.log(ratio) / math.log(T5_MAX_DIST / max_exact) * (T5_BUCKETS - max_exact)).astype(i32)
        bucket = jnp.where(d < max_exact, d, jnp.minimum(large, T5_BUCKETS - 1))
        val = jnp.zeros((n, n), f32)
        for k in range(T5_BUCKETS):
            val = jnp.where(bucket == k, rb_ref[k, h], val)
        ref[0] = val


def bias_tiles(rel_bias):
    n = MOBA_BLOCK
    return pl.pallas_call(
        _bias_tiles_body,
        grid=(N_HEADS,),
        in_specs=[pl.BlockSpec(memory_space=pltpu.SMEM)],
        out_specs=[pl.BlockSpec((1, n, n), lambda h: (h, 0, 0))] * 2,
        out_shape=[jax.ShapeDtypeStruct((N_HEADS, n, n), f32)] * 2,
        compiler_params=_cp(("arbitrary",)),
        name="bias_tiles",
    )(rel_bias)


def _in_proj_body(x_ref, g_ref, w_ref, ws_ref, gain_ref, bd_ref, o_ref, os_ref, h_scr):
    j = pl.program_id(1)

    @pl.when(j == 0)
    def _():
        x = x_ref[...]
        h = x * lax.rsqrt(jnp.mean(x * x, axis=-1, keepdims=True) + RMS_EPS) * g_ref[...]
        hb = h.astype(bf16)
        h_scr[...] = hb
        os_ref[...] = jnp.dot(hb, ws_ref[...], preferred_element_type=f32)

    acc = jnp.dot(h_scr[...], w_ref[...], preferred_element_type=f32)

    @pl.when(j < N_NORMED)
    def _():
        sq = acc * acc
        hi = sq.astype(bf16)
        lo = (sq - hi.astype(f32)).astype(bf16)
        ms = (jnp.dot(hi, bd_ref[...], preferred_element_type=f32)
              + jnp.dot(lo, bd_ref[...], preferred_element_type=f32))
        o_ref[...] = acc * lax.rsqrt(ms + RMS_EPS) * gain_ref[0]

    @pl.when(j >= N_NORMED)
    def _():
        o_ref[...] = acc


def in_proj(x, g_mix, w_main, w_small, gains, tm):
    t = x.shape[0]
    nj = C_END // PROJ_TN
    head_avg = jnp.kron(jnp.eye(N_HEADS, dtype=f32), jnp.full((HEAD_DIM, HEAD_DIM), 1.0 / HEAD_DIM, f32)).astype(bf16)
    return pl.pallas_call(
        _in_proj_body,
        grid=(t // tm, nj),
        in_specs=[
            pl.BlockSpec((tm, D_MODEL), lambda i, j: (i, 0)),
            pl.BlockSpec((1, D_MODEL), lambda i, j: (0, 0)),
            pl.BlockSpec((D_MODEL, PROJ_TN), lambda i, j: (0, j)),
            pl.BlockSpec((D_MODEL, S_W), lambda i, j: (0, 0)),
            pl.BlockSpec((1, 1, PROJ_TN), lambda i, j: (jnp.minimum(j, N_NORMED - 1), 0, 0)),
            pl.BlockSpec((PROJ_TN, PROJ_TN), lambda i, j: (0, 0)),
        ],
        out_specs=[
            pl.BlockSpec((tm, PROJ_TN), lambda i, j: (i, j)),
            pl.BlockSpec((tm, S_W), lambda i, j: (i, 0)),
        ],
        out_shape=[jax.ShapeDtypeStruct((t, C_END), f32), jax.ShapeDtypeStruct((t, S_W), f32)],
        scratch_shapes=[pltpu.VMEM((tm, D_MODEL), bf16)],
        compiler_params=_cp(("arbitrary", "arbitrary")),
        name="in_proj",
    )(x, g_mix.reshape(1, D_MODEL), w_main, w_small, gains, head_avg)


def _logf_body(s_ref, b_ref, lf_ref, cum_ref, carry, *, chunks_per_seq):
    i = pl.program_id(0)
    n = s_ref.shape[0]

    @pl.when(i % chunks_per_seq == 0)
    def _():
        carry[...] = jnp.zeros_like(carry)

    ff = s_ref[:, S_FF:S_FF + N_HEADS] + b_ref[...]
    logf = -_softplus(-ff)
    lf_ref[...] = logf
    row = lax.broadcasted_iota(i32, (n, n), 0)
    col = lax.broadcasted_iota(i32, (n, n), 1)
    tri = (row >= col).astype(f32)
    c = jnp.dot(tri, logf, precision=HI, preferred_element_type=f32) + carry[...]
    cum_ref[...] = c
    carry[...] = c[n - 1:n, :]


def logf_cum(small, f_bias, seq_len, tc=256):
    t = small.shape[0]
    return pl.pallas_call(
        functools.partial(_logf_body, chunks_per_seq=seq_len // tc),
        grid=(t // tc,),
        in_specs=[pl.BlockSpec((tc, S_W), lambda i: (i, 0)), pl.BlockSpec((1, N_HEADS), lambda i: (0, 0))],
        out_specs=[pl.BlockSpec((tc, N_HEADS), lambda i: (i, 0))] * 2,
        out_shape=[jax.ShapeDtypeStruct((t, N_HEADS), f32)] * 2,
        scratch_shapes=[pltpu.VMEM((1, N_HEADS), f32)],
        compiler_params=_cp(("arbitrary",)),
        name="logf_cum",
    )(small, f_bias.reshape(1, N_HEADS))


def _half_masks():
    lane = lax.broadcasted_iota(i32, (1, 2 * HEAD_DIM), 1)
    return [lane < HEAD_DIM, lane >= HEAD_DIM]


def _softmax_step(s, m, l, acc, v_masked):
    m_new = jnp.maximum(m, jnp.max(s, axis=-1, keepdims=True))
    alpha = jnp.exp(m - m_new)
    p = jnp.exp(s - m_new)
    l = l * alpha + jnp.sum(p, axis=-1, keepdims=True)
    acc = acc * alpha + jnp.dot(p.astype(bf16), v_masked, preferred_element_type=f32)
    return m_new, l, acc


def _fox_prompt_body(q_ref, k_ref, v_ref, cq_ref, ck_ref, o_ref, *, tq):
    pair = pl.program_id(1)
    qi = pl.program_id(2)
    half = _half_masks()
    q = q_ref[...] * (HEAD_DIM ** -0.5)
    qh = [jnp.where(half[hh], q, 0.0).astype(bf16) for hh in (0, 1)]
    lane8 = lax.broadcasted_iota(i32, (1, N_HEADS), 1)
    sub8 = lax.broadcasted_iota(i32, (N_HEADS, 1), 0)
    cq_all = cq_ref[...]
    cq = [jnp.sum(jnp.where(lane8 == 2 * pair + hh, cq_all, 0.0), axis=1, keepdims=True) for hh in (0, 1)]
    row = lax.broadcasted_iota(i32, (tq, tq), 0)
    col = lax.broadcasted_iota(i32, (tq, tq), 1)

    def body(kb, carry):
        off = pl.multiple_of(kb * tq, tq)
        k = k_ref[pl.ds(off, tq), :].astype(bf16)
        v = v_ref[pl.ds(off, tq), :]
        ck_all = ck_ref[0, :, pl.ds(off, tq)]
        causal = (kb * tq + col) <= (qi * tq + row)
        out = []
        for hh in (0, 1):
            m, l, acc = carry[hh]
            ck = jnp.sum(jnp.where(sub8 == 2 * pair + hh, ck_all, 0.0), axis=0, keepdims=True)
            s = lax.dot_general(qh[hh], k, NT, preferred_element_type=f32)
            s = s + cq[hh] - ck
            s = jnp.where(causal, s, -jnp.inf)
            vm = jnp.where(half[hh], v, 0.0).astype(bf16)
            out.append(_softmax_step(s, m, l, acc, vm))
        return tuple(out)

    init = tuple((jnp.full((tq, 1), -jnp.inf, f32), jnp.zeros((tq, 1), f32), jnp.zeros((tq, 2 * HEAD_DIM), f32))
                 for _ in (0, 1))
    res = lax.fori_loop(0, qi + 1, body, init)
    o_ref[...] = res[0][2] / res[0][1] + res[1][2] / res[1][1]


def _moba_select(gate, n_valid):
    nblk = gate.shape[1]
    idx = lax.broadcasted_iota(i32, (1, nblk), 1)
    g = jnp.where(idx < n_valid, gate, -jnp.inf)
    rank = jnp.zeros(g.shape, i32)
    for m in range(nblk):
        gm = g[:, m:m + 1]
        rank = rank + ((gm > g) | ((gm == g) & (m < idx))).astype(i32)
    return (rank < MOBA_TOPK) & (idx < n_valid)


def _moba_prompt_body(q_ref, k_ref, v_ref, bd_ref, bp_ref, o_ref, kmean_scr, *, tq, nblk):
    qi = pl.program_id(2)
    half = _half_masks()

    @pl.when(qi == 0)
    def _():
        kmean_scr[...] = jnp.mean(k_ref[...].reshape(nblk, tq, 2 * HEAD_DIM), axis=1)

    q = q_ref[...]
    qs = q * (HEAD_DIM ** -0.5)
    qh = [jnp.where(half[hh], qs, 0.0).astype(bf16) for hh in (0, 1)]
    kmean = kmean_scr[...]
    blk_idx = lax.broadcasted_iota(i32, (1, nblk), 1)
    sel = []
    for hh in (0, 1):
        gate = lax.dot_general(jnp.where(half[hh], q, 0.0), kmean, NT, precision=HI, preferred_element_type=f32)
        sel.append(_moba_select(gate, qi).astype(f32))
    row = lax.broadcasted_iota(i32, (tq, tq), 0)
    col = lax.broadcasted_iota(i32, (tq, tq), 1)
    causal = col <= row

    off = pl.multiple_of(qi * tq, tq)
    k = k_ref[pl.ds(off, tq), :].astype(bf16)
    v = v_ref[pl.ds(off, tq), :]
    state = []
    for hh in (0, 1):
        s = lax.dot_general(qh[hh], k, NT, preferred_element_type=f32) + bd_ref[hh]
        s = jnp.where(causal, s, -jnp.inf)
        vm = jnp.where(half[hh], v, 0.0).astype(bf16)
        state.append(_softmax_step(s, jnp.full((tq, 1), -jnp.inf, f32), jnp.zeros((tq, 1), f32),
                                   jnp.zeros((tq, 2 * HEAD_DIM), f32), vm))

    def body(kb, carry):
        off = pl.multiple_of(kb * tq, tq)
        k = k_ref[pl.ds(off, tq), :].astype(bf16)
        v = v_ref[pl.ds(off, tq), :]
        out = []
        for hh in (0, 1):
            m, l, acc = carry[hh]
            far = bp_ref[hh, tq - 1:tq, 0:1]
            bias = jnp.where(kb == qi - 1, bp_ref[hh], far)
            picked = jnp.sum(jnp.where(blk_idx == kb, sel[hh], 0.0), axis=1, keepdims=True) > 0.5
            s = lax.dot_general(qh[hh], k, NT, preferred_element_type=f32) + bias
            s = jnp.where(picked, s, -jnp.inf)
            vm = jnp.where(half[hh], v, 0.0).astype(bf16)
            out.append(_softmax_step(s, m, l, acc, vm))
        return tuple(out)

    res = lax.fori_loop(0, qi, body, tuple(state))
    o_ref[...] = res[0][2] / res[0][1] + res[1][2] / res[1][1]


def attn_prompt(mode, proj, batch, seq, c_q, c_k, c_v, extra, tq=MOBA_BLOCK):
    nq = seq // tq
    npair = N_HEADS // 2
    pw = 2 * HEAD_DIM
    in_specs = [
        pl.BlockSpec((tq, pw), lambda b, p, i: (b * nq + i, c_q // pw + p)),
        pl.BlockSpec((seq, pw), lambda b, p, i: (b, c_k // pw + p)),
        pl.BlockSpec((seq, pw), lambda b, p, i: (b, c_v // pw + p)),
    ]
    if mode == "fox":
        cum, cum_t = extra
        body = functools.partial(_fox_prompt_body, tq=tq)
        in_specs += [pl.BlockSpec((tq, N_HEADS), lambda b, p, i: (b * nq + i, 0)),
                     pl.BlockSpec((1, N_HEADS, seq), lambda b, p, i: (b, 0, 0))]
        args = (cum, cum_t)
        scratch = []
    else:
        bias_d, bias_p = extra
        body = functools.partial(_moba_prompt_body, tq=tq, nblk=nq)
        in_specs += [pl.BlockSpec((2, tq, tq), lambda b, p, i: (p, 0, 0))] * 2
        args = (bias_d, bias_p)
        scratch = [pltpu.VMEM((nq, pw), f32)]
    return pl.pallas_call(
        body,
        grid=(batch, npair, nq),
        in_specs=in_specs,
        out_specs=pl.BlockSpec((tq, pw), lambda b, p, i: (b * nq + i, p)),
        out_shape=jax.ShapeDtypeStruct((batch * seq, ATT_W), f32),
        scratch_shapes=scratch,
        compiler_params=_cp(("arbitrary", "arbitrary", "arbitrary")),
        name=f"{mode}_prompt",
    )(proj, proj, proj, *args)


def _expand_q(q):
    nq = q.shape[0]
    rows = N_HEADS * nq
    qt = jnp.broadcast_to(q[None], (N_HEADS, nq, ATT_W)).reshape(rows, ATT_W)
    rh = lax.broadcasted_iota(i32, (rows, ATT_W), 0) // nq
    lh = lax.broadcasted_iota(i32, (rows, ATT_W), 1) // HEAD_DIM
    return qt, rh == lh


def _rows_from_heads(x_t, nq):
    return jnp.broadcast_to(x_t[:, None, :], (N_HEADS, nq, x_t.shape[1])).reshape(N_HEADS * nq, x_t.shape[1])


def _fox_sample_body(pt_ref, q_ref, kn_ref, vn_ref, lfn_ref, *refs, pps, nq):
    kp = refs[:pps]
    vp = refs[pps:2 * pps]
    lp = refs[2 * pps:3 * pps]
    o_ref, m_scr, l_scr, acc_scr, carry = refs[3 * pps:]
    step = pl.program_id(1)
    rows = N_HEADS * nq

    @pl.when(step == 0)
    def _():
        m_scr[...] = jnp.full_like(m_scr, -jnp.inf)
        l_scr[...] = jnp.zeros_like(l_scr)
        acc_scr[...] = jnp.zeros_like(acc_scr)
        carry[...] = jnp.zeros_like(carry)

    qt, diag = _expand_q(q_ref[...] * (HEAD_DIM ** -0.5))
    qe = jnp.where(diag, qt, 0.0).astype(bf16)
    eye = (lax.broadcasted_iota(i32, (N_HEADS, N_HEADS), 0) == lax.broadcasted_iota(i32, (N_HEADS, N_HEADS), 1)).astype(f32)

    def cum_t(logf):
        n = logf.shape[0]
        tri = (lax.broadcasted_iota(i32, (n, n), 0) >= lax.broadcasted_iota(i32, (n, n), 1)).astype(f32)
        c = jnp.dot(tri, logf, precision=HI, preferred_element_type=f32) + carry[...]
        carry[...] = c[n - 1:n, :]
        return lax.dot_general(eye, c, NT, precision=HI, preferred_element_type=f32)

    def update(s, v):
        m, l, acc = _softmax_step(s, m_scr[...], l_scr[...], acc_scr[...], v.astype(bf16))
        m_scr[...] = m
        l_scr[...] = l
        acc_scr[...] = acc

    for j in range(pps):
        s = lax.dot_general(qe, kp[j][0].astype(bf16), NT, preferred_element_type=f32)
        s = s - _rows_from_heads(cum_t(lp[j][0]), nq)
        update(s, vp[j][0])

    @pl.when(step == pl.num_programs(1) - 1)
    def _():
        s = lax.dot_general(qe, kn_ref[...].astype(bf16), NT, preferred_element_type=f32)
        s = s - _rows_from_heads(cum_t(lfn_ref[...]), nq)
        qpos = lax.broadcasted_iota(i32, (rows, nq), 0) % nq
        kpos = lax.broadcasted_iota(i32, (rows, nq), 1)
        s = jnp.where(kpos <= qpos, s, -jnp.inf)
        update(s, vn_ref[...])
        out = jnp.where(diag, acc_scr[...] / l_scr[...], 0.0)
        o_ref[...] = jnp.sum(out.reshape(N_HEADS, nq, ATT_W), axis=0)


def _moba_sample_body(pt_ref, q_ref, kn_ref, vn_ref, bo_ref, bpast_ref, *refs, pps, nq):
    kp = refs[:pps]
    vp = refs[pps:2 * pps]
    o_ref, m_scr, l_scr, acc_scr, ksum_scr = refs[2 * pps:]
    step = pl.program_id(1)
    nsteps = pl.num_programs(1)
    rows = N_HEADS * nq
    nblk = m_scr.shape[0]

    q = q_ref[...]
    qt, diag = _expand_q(q)
    qe = jnp.where(diag, qt * (HEAD_DIM ** -0.5), 0.0).astype(bf16)

    k = jnp.concatenate([kp[j][0] for j in range(pps)], axis=0)
    v = jnp.concatenate([vp[j][0] for j in range(pps)], axis=0)
    ksum_scr[pl.ds(step, 1), :] = jnp.sum(k, axis=0, keepdims=True)
    s = lax.dot_general(qe, k.astype(bf16), NT, preferred_element_type=f32) + bpast_ref[0]
    m = jnp.max(s, axis=-1, keepdims=True)
    p = jnp.exp(s - m)
    m_scr[step] = m
    l_scr[step] = jnp.sum(p, axis=-1, keepdims=True)
    acc_scr[step] = jnp.dot(p.astype(bf16), v.astype(bf16), preferred_element_type=f32)

    @pl.when(step == nsteps - 1)
    def _():
        kmean = ksum_scr[...] * (1.0 / MOBA_BLOCK)
        gate = lax.dot_general(jnp.where(diag, qt, 0.0), kmean, NT, precision=HI, preferred_element_type=f32)
        sel = _moba_select(gate, nblk).astype(f32)
        picked = [sel[:, n:n + 1] > 0.5 for n in range(nblk)]
        s_new = lax.dot_general(qe, kn_ref[...].astype(bf16), NT, preferred_element_type=f32) + bo_ref[...]
        qpos = lax.broadcasted_iota(i32, (rows, nq), 0) % nq
        kpos = lax.broadcasted_iota(i32, (rows, nq), 1)
        s_new = jnp.where(kpos <= qpos, s_new, -jnp.inf)
        m_tot = jnp.max(s_new, axis=-1, keepdims=True)
        for n in range(nblk):
            m_tot = jnp.maximum(m_tot, jnp.where(picked[n], m_scr[n], -jnp.inf))
        p_new = jnp.exp(s_new - m_tot)
        l_tot = jnp.sum(p_new, axis=-1, keepdims=True)
        acc = jnp.dot(p_new.astype(bf16), vn_ref[...].astype(bf16), preferred_element_type=f32)
        for n in range(nblk):
            w = jnp.where(picked[n], jnp.exp(m_scr[n] - m_tot), 0.0)
            l_tot = l_tot + w * l_scr[n]
            acc = acc + w * acc_scr[n]
        out = jnp.where(diag, acc / l_tot, 0.0)
        o_ref[...] = jnp.sum(out.reshape(N_HEADS, nq, ATT_W), axis=0)


def attn_sample(mode, proj, row0, n_seq, nq, c_q, c_k, c_v, pools, page_table, layer, extra):
    k_pool, v_pool = pools[:2]
    n_pool = k_pool.shape[1]
    n_pages = page_table.shape[1]
    pps = MOBA_BLOCK // PAGE
    nsteps = n_pages // pps
    kflat = k_pool.reshape(-1, PAGE, ATT_W)
    vflat = v_pool.reshape(-1, PAGE, ATT_W)
    rb0 = row0 // nq

    def new_spec(c0):
        return pl.BlockSpec((nq, ATT_W), lambda b, s, pt: (rb0 + b, c0 // ATT_W))

    def page_spec(j, width):
        return pl.BlockSpec((1, PAGE, width), lambda b, s, pt: (layer * n_pool + pt[b, s * pps + j], 0, 0))

    rows = N_HEADS * nq
    if mode == "fox":
        logf_new, lf_pool = extra
        lflat = lf_pool.reshape(-1, PAGE, N_HEADS)
        body = functools.partial(_fox_sample_body, pps=pps, nq=nq)
        in_specs = ([new_spec(c_q), new_spec(c_k), new_spec(c_v),
                     pl.BlockSpec((nq, N_HEADS), lambda b, s, pt: (rb0 + b, 0))]
                    + [page_spec(j, ATT_W) for j in range(pps)] * 2 + [page_spec(j, N_HEADS) for j in range(pps)])
        args = (proj, proj, proj, logf_new) + (kflat,) * pps + (vflat,) * pps + (lflat,) * pps
        scratch = [pltpu.VMEM((rows, 1), f32), pltpu.VMEM((rows, 1), f32), pltpu.VMEM((rows, ATT_W), f32),
                   pltpu.VMEM((1, N_HEADS), f32)]
    else:
        bias_own, bias_past = extra
        body = functools.partial(_moba_sample_body, pps=pps, nq=nq)
        in_specs = ([new_spec(c_q), new_spec(c_k), new_spec(c_v),
                     pl.BlockSpec((rows, nq), lambda b, s, pt: (0, 0)),
                     pl.BlockSpec((1, rows, MOBA_BLOCK), lambda b, s, pt: (s, 0, 0))]
                    + [page_spec(j, ATT_W) for j in range(pps)] * 2)
        args = (proj, proj, proj, bias_own, bias_past) + (kflat,) * pps + (vflat,) * pps
        scratch = [pltpu.VMEM((nsteps, rows, 1), f32), pltpu.VMEM((nsteps, rows, 1), f32),
                   pltpu.VMEM((nsteps, rows, ATT_W), f32), pltpu.VMEM((nsteps, ATT_W), f32)]
    return pl.pallas_call(
        body,
        grid_spec=pltpu.PrefetchScalarGridSpec(
            num_scalar_prefetch=1,
            grid=(n_seq, nsteps),
            in_specs=in_specs,
            out_specs=pl.BlockSpec((nq, ATT_W), lambda b, s, pt: (b, 0)),
            scratch_shapes=scratch,
        ),
        out_shape=jax.ShapeDtypeStruct((n_seq * nq, ATT_W), f32),
        compiler_params=_cp(("arbitrary", "arbitrary")),
        name=f"{mode}_sample",
    )(page_table, *args)


def _ssd_body(z_ref, xs_ref, bc_ref, sm_ref, cs_ref, h0_ref, cw_ref, cb_ref, dtb_ref, alog_ref, dsk_ref, ng_ref,
              y_ref, conv_ref, ssm_ref, xp_scr, ht_scr, yd_scr, *, cl):
    c = pl.program_id(1)
    nc = pl.num_programs(1)
    tail = SSM_CONV - 1
    base = 8
    nh, hp, ns = SSM_HEADS, SSM_P, SSM_N
    gw = SSM_INNER // SSM_GROUPS

    @pl.when(c == 0)
    def _():
        xp_scr[base - tail:base, :] = cs_ref[0]
        ht_scr[...] = h0_ref[0].T

    xp_scr[base:base + cl, 0:SSM_INNER] = xs_ref[...]
    xp_scr[base:base + cl, SSM_INNER:CONV_DIM] = bc_ref[...]
    conv = cb_ref[...]
    for w in range(SSM_CONV):
        conv = conv + xp_scr[pl.ds(base - tail + w, cl), :] * cw_ref[w:w + 1, :]
    new_tail = xp_scr[base + cl - tail:base + cl, :]
    xp_scr[base - tail:base, :] = new_tail
    u = _silu(conv)
    xs = u[:, :SSM_INNER]
    bm = [u[:, SSM_INNER + g * ns:SSM_INNER + (g + 1) * ns] for g in range(SSM_GROUPS)]
    cm = [u[:, SSM_INNER + (SSM_GROUPS + g) * ns:SSM_INNER + (SSM_GROUPS + g + 1) * ns] for g in range(SSM_GROUPS)]

    dt = _softplus(sm_ref[:, S_DT:S_DT + nh] + dtb_ref[...])
    a = dt * (-jnp.exp(alog_ref[...]))
    row = lax.broadcasted_iota(i32, (cl, cl), 0)
    col = lax.broadcasted_iota(i32, (cl, cl), 1)
    causal = row >= col
    acum = jnp.dot(causal.astype(f32), a, precision=HI, preferred_element_type=f32)
    eye = (lax.broadcasted_iota(i32, (nh, nh), 0) == lax.broadcasted_iota(i32, (nh, nh), 1)).astype(f32)
    acum_t = lax.dot_general(eye, acum, NT, precision=HI, preferred_element_type=f32)
    a_end = acum[cl - 1:cl, :]
    spread = (lax.broadcasted_iota(i32, (nh, SSM_INNER), 1) // hp == lax.broadcasted_iota(i32, (nh, SSM_INNER), 0)).astype(f32)

    def lanes(x):
        return jnp.dot(x, spread, precision=HI, preferred_element_type=f32)

    xdt = xs * lanes(dt)
    half = _half_masks()

    cb = [lax.dot_general(cm[g].astype(bf16), bm[g].astype(bf16), NT, preferred_element_type=f32) for g in range(SSM_GROUPS)]
    for pr in range(nh // 2):
        xp_pair = xdt[:, pr * 2 * hp:(pr + 1) * 2 * hp]
        acc = None
        for hh in (0, 1):
            hd = 2 * pr + hh
            seg = acum[:, hd:hd + 1] - acum_t[hd:hd + 1, :]
            decay = jnp.exp(jnp.where(causal, seg, -jnp.inf))
            mm = (cb[hd // (nh // SSM_GROUPS)] * decay).astype(bf16)
            t = jnp.dot(mm, jnp.where(half[hh], xp_pair, 0.0).astype(bf16), preferred_element_type=f32)
            acc = t if acc is None else acc + t
        yd_scr[:, pr * 2 * hp:(pr + 1) * 2 * hp] = acc

    ht = ht_scr[...]
    xw = (xdt * lanes(jnp.exp(a_end - acum))).astype(bf16)
    y_off = []
    st = []
    for g in range(SSM_GROUPS):
        y_off.append(jnp.dot(cm[g].astype(bf16), ht[:, g * gw:(g + 1) * gw].astype(bf16), preferred_element_type=f32))
        st.append(lax.dot_general(bm[g].astype(bf16), xw[:, g * gw:(g + 1) * gw], TN, preferred_element_type=f32))
    from_start = lanes(jnp.exp(acum))
    y = yd_scr[...] + jnp.concatenate(y_off, axis=1) * from_start + xs * dsk_ref[...]
    ht_new = ht * from_start[cl - 1:cl, :] + jnp.concatenate(st, axis=1)
    ht_scr[...] = ht_new

    yg = y * _silu(z_ref[...])
    outs = []
    for g in range(SSM_GROUPS):
        part = yg[:, g * gw:(g + 1) * gw]
        outs.append(part * lax.rsqrt(jnp.mean(part * part, axis=-1, keepdims=True) + RMS_EPS))
    y_ref[...] = jnp.concatenate(outs, axis=1) * ng_ref[...]

    @pl.when(c == nc - 1)
    def _():
        conv_ref[0] = new_tail
        ssm_ref[0] = ht_new.T


def ssd(proj, small, row0, n_seq, seq, cl, conv_state, ssm_state, conv_w, conv_b, dt_bias, a_log, d_skip, norm_g):
    nc = seq // cl
    rb0 = row0 // cl
    h0 = ssm_state.reshape(n_seq, SSM_HEADS * SSM_P, SSM_N)

    def rows(width, c0):
        return pl.BlockSpec((cl, width), lambda b, c: (rb0 + b * nc + c, c0 // width))

    def const(shape):
        return pl.BlockSpec(shape, lambda b, c: (0,) * len(shape))

    y, new_conv, new_ssm = pl.pallas_call(
        functools.partial(_ssd_body, cl=cl),
        grid=(n_seq, nc),
        in_specs=[
            rows(SSM_INNER, C_Z), rows(SSM_INNER, C_XS), rows(CONV_DIM - SSM_INNER, C_BC), rows(S_W, 0),
            pl.BlockSpec((1, SSM_CONV - 1, CONV_DIM), lambda b, c: (b, 0, 0)),
            pl.BlockSpec((1, SSM_HEADS * SSM_P, SSM_N), lambda b, c: (b, 0, 0)),
            const((SSM_CONV, CONV_DIM)), const((1, CONV_DIM)), const((1, SSM_HEADS)), const((1, SSM_HEADS)),
            const((1, SSM_INNER)), const((1, SSM_INNER)),
        ],
        out_specs=[
            pl.BlockSpec((cl, SSM_INNER), lambda b, c: (b * nc + c, 0)),
            pl.BlockSpec((1, SSM_CONV - 1, CONV_DIM), lambda b, c: (b, 0, 0)),
            pl.BlockSpec((1, SSM_HEADS * SSM_P, SSM_N), lambda b, c: (b, 0, 0)),
        ],
        out_shape=[
            jax.ShapeDtypeStruct((n_seq * seq, SSM_INNER), f32),
            jax.ShapeDtypeStruct((n_seq, SSM_CONV - 1, CONV_DIM), f32),
            jax.ShapeDtypeStruct((n_seq, SSM_HEADS * SSM_P, SSM_N), f32),
        ],
        scratch_shapes=[pltpu.VMEM((8 + cl, CONV_DIM), f32), pltpu.VMEM((SSM_N, SSM_INNER), f32),
                        pltpu.VMEM((cl, SSM_INNER), f32)],
        compiler_params=_cp(("arbitrary", "arbitrary")),
        name=f"ssd_cl{cl}",
    )(proj, proj, proj, small, conv_state, h0, conv_w, conv_b.reshape(1, CONV_DIM), dt_bias.reshape(1, SSM_HEADS),
      a_log.reshape(1, SSM_HEADS), jnp.repeat(d_skip, SSM_P).reshape(1, SSM_INNER), norm_g.reshape(1, SSM_INNER))
    return y, new_conv, new_ssm.reshape(n_seq, SSM_HEADS, SSM_P, SSM_N)


def _merge_body(x_ref, ya_ref, ym_ref, yc_ref, ga_ref, gb_ref, gc_ref, wa_ref, wb_ref, wc_ref, wo_ref, o_ref):
    def branch(y_ref, w_ref, g_ref):
        return jax.nn.sigmoid(g_ref[...]) * jnp.dot(y_ref[...].astype(bf16), w_ref[...], preferred_element_type=f32)

    merged = branch(ya_ref, wa_ref, ga_ref) + branch(ym_ref, wb_ref, gb_ref) + branch(yc_ref, wc_ref, gc_ref)
    o_ref[...] = x_ref[...] + jnp.dot(merged.astype(bf16), wo_ref[...], preferred_element_type=f32)


def merge(x, ya, ym, yc, proj, wa, wb, wc, wo, tm):
    t = x.shape[0]
    g0 = C_GL // D_MODEL

    def rows(width, cb=0):
        return pl.BlockSpec((tm, width), lambda i: (i, cb))

    def const(shape):
        return pl.BlockSpec(shape, lambda i: (0, 0))

    return pl.pallas_call(
        _merge_body,
        grid=(t // tm,),
        in_specs=[rows(D_MODEL), rows(ATT_W), rows(ATT_W), rows(SSM_INNER),
                  rows(D_MODEL, g0), rows(D_MODEL, g0 + 1), rows(D_MODEL, g0 + 2),
                  const((ATT_W, D_MODEL)), const((ATT_W, D_MODEL)), const((SSM_INNER, D_MODEL)), const((D_MODEL, D_MODEL))],
        out_specs=rows(D_MODEL),
        out_shape=jax.ShapeDtypeStruct((t, D_MODEL), f32),
        compiler_params=_cp(("arbitrary",)),
        name="merge",
    )(x, ya, ym, yc, proj, proj, proj, wa, wb, wc, wo)


def _router_body(x_ref, g_ref, wr_ref, br_ref, h_ref, eid_ref, rank_ref, ewt_ref, cnt_ref, cnt_scr, *, tm):
    i = pl.program_id(0)

    @pl.when(i == 0)
    def _():
        cnt_scr[...] = jnp.zeros_like(cnt_scr)

    x = x_ref[...]
    h = x * lax.rsqrt(jnp.mean(x * x, axis=-1, keepdims=True) + RMS_EPS) * g_ref[...]
    h_ref[...] = h
    logits = lax.dot_general(wr_ref[...], h, NT, precision=HI, preferred_element_type=f32) + br_ref[...]
    sub = lax.broadcasted_iota(i32, (8, tm), 0)
    gl = jnp.where(sub < N_GROUPS_E, logits[0:8], -jnp.inf)
    gmax = jnp.max(gl, axis=0, keepdims=True)
    g_top = 1.0 / jnp.sum(jnp.exp(gl - gmax), axis=0, keepdims=True)
    g_idx = jnp.min(jnp.where(gl == gmax, sub, 8), axis=0, keepdims=True)
    e_in = jnp.zeros((E_PER_GROUP, tm), f32)
    for g in range(N_GROUPS_E):
        e_in = jnp.where(g_idx == g, logits[8 + g * E_PER_GROUP:8 + (g + 1) * E_PER_GROUP], e_in)
    ex = jnp.exp(e_in - jnp.max(e_in, axis=0, keepdims=True))
    prob = ex / jnp.sum(ex, axis=0, keepdims=True)
    p1 = jnp.max(prob, axis=0, keepdims=True)
    i1 = jnp.min(jnp.where(prob == p1, sub, 8), axis=0, keepdims=True)
    rest = jnp.where(sub == i1, -1.0, prob)
    p2 = jnp.max(rest, axis=0, keepdims=True)
    i2 = jnp.min(jnp.where(rest == p2, sub, 8), axis=0, keepdims=True)
    denom = p1 + p2
    ids = [g_idx * E_PER_GROUP + i1, g_idx * E_PER_GROUP + i2]
    wts = [g_top * p1 / denom, g_top * p2 / denom]
    eid_ref[...] = jnp.concatenate(ids, axis=0)
    ewt_ref[...] = jnp.concatenate(wts + [jnp.zeros((6, tm), f32)], axis=0).T

    esub = lax.broadcasted_iota(i32, (N_EXPERTS, tm), 0)
    oh = [(esub == ids[k]).astype(f32) for k in (0, 1)]
    both = oh[0] + oh[1]
    before = (lax.broadcasted_iota(i32, (tm, tm), 0) < lax.broadcasted_iota(i32, (tm, tm), 1)).astype(bf16)
    pos = jnp.dot(both.astype(bf16), before, preferred_element_type=f32) + cnt_scr[:, 0:1]
    rank_ref[...] = jnp.concatenate([jnp.sum(oh[k] * pos, axis=0, keepdims=True) for k in (0, 1)], axis=0).astype(i32)
    cnt_scr[...] = cnt_scr[...] + jnp.sum(both, axis=1, keepdims=True)
    cnt_ref[...] = cnt_scr[...].astype(i32)


def router(x, g_ffn, w_router_t, b_router, tm):
    t = x.shape[0]
    nr = w_router_t.shape[0]
    return pl.pallas_call(
        functools.partial(_router_body, tm=tm),
        grid=(t // tm,),
        in_specs=[pl.BlockSpec((tm, D_MODEL), lambda i: (i, 0)), pl.BlockSpec((1, D_MODEL), lambda i: (0, 0)),
                  pl.BlockSpec((nr, D_MODEL), lambda i: (0, 0)), pl.BlockSpec((nr, 1), lambda i: (0, 0))],
        out_specs=[pl.BlockSpec((tm, D_MODEL), lambda i: (i, 0)),
                   pl.BlockSpec((2, tm), lambda i: (0, i)), pl.BlockSpec((2, tm), lambda i: (0, i)),
                   pl.BlockSpec((tm, 8), lambda i: (i, 0)),
                   pl.BlockSpec((N_EXPERTS, 128), lambda i: (0, 0))],
        out_shape=[jax.ShapeDtypeStruct((t, D_MODEL), f32),
                   jax.ShapeDtypeStruct((2, t), i32), jax.ShapeDtypeStruct((2, t), i32),
                   jax.ShapeDtypeStruct((t, 8), f32),
                   jax.ShapeDtypeStruct((N_EXPERTS, 128), i32)],
        scratch_shapes=[pltpu.VMEM((N_EXPERTS, 128), f32)],
        compiler_params=_cp(("arbitrary",)),
        name="router",
    )(x, g_ffn.reshape(1, D_MODEL), w_router_t, b_router)


def _dispatch_body(pstart_ref, eid_ref, rank_ref, h_ref, xs_in_ref, xs_ref, dest_ref, sem, *, tm):
    del xs_in_ref

    def copy(t, d):
        return pltpu.make_async_copy(h_ref.at[pl.ds(t, 1), :], xs_ref.at[pl.ds(d, 1), :], sem)

    def issue(t, _):
        for k in (0, 1):
            d = pstart_ref[eid_ref[k, t]] + rank_ref[k, t]
            dest_ref[k, t] = d
            copy(t, d).start()
        return 0

    lax.fori_loop(0, tm, issue, 0)

    def drain(t, _):
        for k in (0, 1):
            copy(t, dest_ref[k, t]).wait()
        return 0

    lax.fori_loop(0, tm, drain, 0)


def dispatch(h, eid, rank, pstart, cap, tm):
    t = h.shape[0]
    smem_rows = pl.BlockSpec((2, tm), lambda i, ps: (0, i), memory_space=pltpu.SMEM)
    return pl.pallas_call(
        functools.partial(_dispatch_body, tm=tm),
        grid_spec=pltpu.PrefetchScalarGridSpec(
            num_scalar_prefetch=1,
            grid=(t // tm,),
            in_specs=[smem_rows, smem_rows, pl.BlockSpec((tm, D_MODEL), lambda i, ps: (i, 0)),
                      pl.BlockSpec(memory_space=pl.ANY)],
            out_specs=[pl.BlockSpec(memory_space=pl.ANY), smem_rows],
            scratch_shapes=[pltpu.SemaphoreType.DMA(())],
        ),
        out_shape=[jax.ShapeDtypeStruct((cap, D_MODEL), f32), jax.ShapeDtypeStruct((2, t), i32)],
        input_output_aliases={4: 0},
        compiler_params=_cp(("arbitrary",)),
        name="moe_dispatch",
    )(pstart, eid, rank, h, jnp.zeros((cap, D_MODEL), f32))


def _experts_body(be_ref, x_ref, w1_ref, w3_ref, w2_ref, o_ref):
    xb = x_ref[...].astype(bf16)
    a = jnp.dot(xb, w1_ref[0], preferred_element_type=f32)
    b = jnp.dot(xb, w3_ref[0], preferred_element_type=f32)
    o_ref[...] = jnp.dot((_silu(a) * b).astype(bf16), w2_ref[0], preferred_element_type=f32)


def experts(xs, blk_expert, w1, w3, w2, blk):
    cap = xs.shape[0]
    return pl.pallas_call(
        _experts_body,
        grid_spec=pltpu.PrefetchScalarGridSpec(
            num_scalar_prefetch=1,
            grid=(cap // blk,),
            in_specs=[pl.BlockSpec((blk, D_MODEL), lambda i, be: (i, 0)),
                      pl.BlockSpec((1, D_MODEL, D_EXPERT), lambda i, be: (be[i], 0, 0)),
                      pl.BlockSpec((1, D_MODEL, D_EXPERT), lambda i, be: (be[i], 0, 0)),
                      pl.BlockSpec((1, D_EXPERT, D_MODEL), lambda i, be: (be[i], 0, 0))],
            out_specs=pl.BlockSpec((blk, D_MODEL), lambda i, be: (i, 0)),
        ),
        out_shape=jax.ShapeDtypeStruct((cap, D_MODEL), f32),
        compiler_params=_cp(("arbitrary",)),
        name="moe_experts",
    )(blk_expert, xs, w1, w3, w2)


def _combine_body(dest_ref, x_ref, ewt_ref, ys_ref, o_ref, buf, sem, *, tm):
    def copy(t, k):
        return pltpu.make_async_copy(ys_ref.at[pl.ds(dest_ref[k, t], 1), :], buf.at[k, pl.ds(t, 1), :], sem)

    def issue(t, _):
        for k in (0, 1):
            copy(t, k).start()
        return 0

    lax.fori_loop(0, tm, issue, 0)

    def drain(t, _):
        for k in (0, 1):
            copy(t, k).wait()
        return 0

    lax.fori_loop(0, tm, drain, 0)
    w = ewt_ref[...]
    o_ref[...] = x_ref[...] + (buf[0] * w[:, 0:1] + buf[1] * w[:, 1:2])


def combine(x, ewt, dest, ys, tm):
    t = x.shape[0]
    return pl.pallas_call(
        functools.partial(_combine_body, tm=tm),
        grid=(t // tm,),
        in_specs=[pl.BlockSpec((2, tm), lambda i: (0, i), memory_space=pltpu.SMEM),
                  pl.BlockSpec((tm, D_MODEL), lambda i: (i, 0)), pl.BlockSpec((tm, 8), lambda i: (i, 0)),
                  pl.BlockSpec(memory_space=pl.ANY)],
        out_specs=pl.BlockSpec((tm, D_MODEL), lambda i: (i, 0)),
        out_shape=jax.ShapeDtypeStruct((t, D_MODEL), f32),
        scratch_shapes=[pltpu.VMEM((2, tm, D_MODEL), f32), pltpu.SemaphoreType.DMA(())],
        compiler_params=_cp(("arbitrary",)),
        name="moe_combine",
    )(dest, x, ewt, ys)


def moe(x, g_ffn, w_router_t, b_router, w1, w3, w2, tm_router, tm_rows, blk):
    t = x.shape[0]
    h, eid, rank, ewt, counts = router(x, g_ffn, w_router_t, b_router, tm_router)
    counts = counts[:, 0]
    padded = (counts + blk - 1) // blk * blk
    pend = jnp.cumsum(padded)
    pstart = (pend - padded).astype(i32)
    nblocks = -(-2 * t // blk) + N_EXPERTS
    blk_expert = jnp.minimum(jnp.searchsorted(pend, jnp.arange(nblocks, dtype=i32) * blk, side="right"),
                             N_EXPERTS - 1).astype(i32)
    xs, dest = dispatch(h, eid, rank, pstart, nblocks * blk, tm_rows)
    ys = experts(xs, blk_expert, w1, w3, w2, blk)
    return combine(x, ewt, dest, ys, tm_rows)


def _prep_layer(l, w_in, fox_q_gain, fox_k_gain, moba_q_gain, moba_k_gain, router_group_w, router_group_b,
                router_expert_w, router_expert_b):
    w = w_in[l]
    o = [0]
    for s in (ATT_W, ATT_W, ATT_W, N_HEADS, ATT_W, ATT_W, ATT_W, SSM_INNER, CONV_DIM, SSM_HEADS, 3 * D_MODEL):
        o.append(o[-1] + s)
    fq, fk, fv, ff, mq, mk, mv, z, xbc, dtr, gl = [w[:, o[i]:o[i + 1]] for i in range(11)]
    w_main = jnp.concatenate([fq, fk, mq, mk, fv, mv, z, xbc[:, :SSM_INNER], gl, xbc[:, SSM_INNER:]], axis=1).astype(bf16)
    w_small = jnp.concatenate([dtr, ff, jnp.zeros((D_MODEL, S_W - SSM_HEADS - N_HEADS), f32)], axis=1).astype(bf16)
    gains = jnp.stack([jnp.tile(g[l], N_HEADS) for g in (fox_q_gain, fox_k_gain, moba_q_gain, moba_k_gain)]).reshape(4, 1, ATT_W)
    w_router_t = jnp.concatenate([router_group_w[l].T, jnp.zeros((8 - N_GROUPS_E, D_MODEL), f32), router_expert_w[l].T], axis=0)
    b_router = jnp.concatenate([router_group_b[l], jnp.zeros((8 - N_GROUPS_E,), f32), router_expert_b[l]]).reshape(-1, 1)
    return w_main, w_small, gains, w_router_t, b_router


def kernel(x_prompt, x_sample, cache_fox_k, cache_fox_v, cache_fox_logf, cache_moba_k, cache_moba_v, state_conv, state_ssm, page_table, rel_bias, g_mix, w_in, fox_q_gain, fox_k_gain, fox_f_bias, moba_q_gain, moba_k_gain, conv_w, conv_b, dt_bias, a_log, d_skip, ssm_norm_g, w_out_fox, w_out_moba, w_out_ssm, w_o, g_ffn, router_group_w, router_group_b, router_expert_w, router_expert_b, expert_w1, expert_w3, expert_w2):
    bp, lp, _ = x_prompt.shape
    bs, ls, _ = x_sample.shape
    tp, ts = bp * lp, bs * ls
    depth = w_in.shape[0]
    x = jnp.concatenate([x_prompt.reshape(tp, D_MODEL), x_sample.reshape(ts, D_MODEL)], axis=0)

    bias_d, bias_p = bias_tiles(rel_bias)
    rows = N_HEADS * ls
    bias_own = bias_d[:, :ls, :ls].reshape(rows, ls)
    n_past_blk = page_table.shape[1] * PAGE // MOBA_BLOCK
    far = jnp.broadcast_to(bias_p[:, MOBA_BLOCK - 1:, :1], (N_HEADS, ls, MOBA_BLOCK)).reshape(1, rows, MOBA_BLOCK)
    bias_past = jnp.concatenate([jnp.broadcast_to(far, (n_past_blk - 1, rows, MOBA_BLOCK)),
                                 bias_p[:, :ls, :].reshape(1, rows, MOBA_BLOCK)], axis=0)
    zero_conv = jnp.zeros((bp, SSM_CONV - 1, CONV_DIM), f32)
    zero_ssm = jnp.zeros((bp, SSM_HEADS, SSM_P, SSM_N), f32)

    new_p = [[] for _ in range(7)]
    new_s = [[] for _ in range(7)]
    for l in range(depth):
        w_main, w_small, gains, w_router_t, b_router = _prep_layer(
            l, w_in, fox_q_gain, fox_k_gain, moba_q_gain, moba_k_gain, router_group_w, router_group_b,
            router_expert_w, router_expert_b)
        proj, small = in_proj(x, g_mix[l], w_main, w_small, gains, tm=512)
        logf, cum = logf_cum(small, fox_f_bias[l], lp)
        cum_t = cum[:tp].reshape(bp, lp, N_HEADS).transpose(0, 2, 1)

        ya_p = attn_prompt("fox", proj, bp, lp, C_FQ, C_FK, C_FV, (cum, cum_t))
        ym_p = attn_prompt("moba", proj, bp, lp, C_MQ, C_MK, C_MV, (bias_d, bias_p))
        ya_s = attn_sample("fox", proj, tp, bs, ls, C_FQ, C_FK, C_FV, (cache_fox_k, cache_fox_v), page_table, l,
                           (logf, cache_fox_logf))
        ym_s = attn_sample("moba", proj, tp, bs, ls, C_MQ, C_MK, C_MV, (cache_moba_k, cache_moba_v), page_table, l,
                           (bias_own, bias_past))
        ssd_w = (conv_w[l], conv_b[l], dt_bias[l], a_log[l], d_skip[l], ssm_norm_g[l])
        yc_p, conv_p, ssm_p = ssd(proj, small, 0, bp, lp, math.gcd(lp, 128), zero_conv, zero_ssm, *ssd_w)
        yc_s, conv_s, ssm_s = ssd(proj, small, tp, bs, ls, math.gcd(ls, 128), state_conv[l], state_ssm[l], *ssd_w)

        x = merge(x, jnp.concatenate([ya_p, ya_s]), jnp.concatenate([ym_p, ym_s]), jnp.concatenate([yc_p, yc_s]), proj,
                  w_out_fox[l].astype(bf16), w_out_moba[l].astype(bf16), w_out_ssm[l].astype(bf16), w_o[l].astype(bf16),
                  tm=256)
        x = moe(x, g_ffn[l], w_router_t, b_router, expert_w1[l].astype(bf16), expert_w3[l].astype(bf16),
                expert_w2[l].astype(bf16), tm_router=512, tm_rows=256, blk=256)

        def heads(c0, r0, b, s):
            return proj[r0:r0 + b * s, c0:c0 + ATT_W].reshape(b, s, N_HEADS, HEAD_DIM)

        for lst, r0, b, s, conv_n, ssm_n in ((new_p, 0, bp, lp, conv_p, ssm_p), (new_s, tp, bs, ls, conv_s, ssm_s)):
            lst[0].append(heads(C_FK, r0, b, s))
            lst[1].append(heads(C_FV, r0, b, s))
            lst[2].append(logf[r0:r0 + b * s].reshape(b, s, N_HEADS))
            lst[3].append(heads(C_MK, r0, b, s))
            lst[4].append(heads(C_MV, r0, b, s))
            lst[5].append(conv_n)
            lst[6].append(ssm_n)

    yp = x[:tp].reshape(bp, lp, D_MODEL)
    ys = x[tp:].reshape(bs, ls, D_MODEL)
    return (yp, ys) + tuple(jnp.stack(a) for a in new_p) + tuple(jnp.stack(a) for a in new_s)
```

```python
import functools
import math

import jax
import jax.numpy as jnp
from jax import lax
from jax.experimental import pallas as pl
from jax.experimental.pallas import tpu as pltpu

f32, bf16, i32 = jnp.float32, jnp.bfloat16, jnp.int32
HI = lax.Precision.HIGHEST
NT = (((1,), (1,)), ((), ()))
TN = (((0,), (0,)), ((), ()))

D_MODEL = 1024
HEAD_DIM = 64
N_HEADS = 8
ATT_W = N_HEADS * HEAD_DIM
ATT_SCALE = HEAD_DIM ** -0.5
PAGE = 128
MOBA_BLOCK = 256
MOBA_TOPK = 3
T5_BUCKETS = 32
T5_MAX_DIST = 128
SSM_HEADS = 16
SSM_P = 64
SSM_N = 128
SSM_GROUPS = 2
SSM_INNER = 1024
SSM_CONV = 4
CONV_DIM = 1536
N_GROUPS_E = 4
E_PER_GROUP = 8
N_EXPERTS = 32
D_EXPERT = 512
RMS_EPS = 1e-6
VMEM_LIMIT = 56 * 1024 * 1024

C_FQ, C_FK, C_MQ, C_MK, C_FV, C_MV, C_Z, C_XS, C_GL, C_BC, C_END = (
    0, 512, 1024, 1536, 2048, 2560, 3072, 4096, 5120, 8192, 8704)
N_NORMED = 4
N_ATT_TILES = 6
PROJ_TN = 512
S_DT, S_FF, S_W = 0, 16, 128


def _cp(sem):
    return pltpu.CompilerParams(dimension_semantics=sem, vmem_limit_bytes=VMEM_LIMIT)


def _softplus(x):
    return jnp.maximum(x, 0.0) + jnp.log1p(jnp.exp(-jnp.abs(x)))


def _silu(x):
    return x * (1.0 / (1.0 + jnp.exp(-x)))


def _bias_tiles_body(rb_ref, d_ref, p_ref):
    h = pl.program_id(0)
    n = MOBA_BLOCK
    row = lax.broadcasted_iota(i32, (n, n), 0)
    col = lax.broadcasted_iota(i32, (n, n), 1)
    max_exact = T5_BUCKETS // 2
    vals = []
    for off in (0, n):
        d = jnp.maximum(row - col + off, 0)
        ratio = jnp.maximum(d, 1).astype(f32) / max_exact
        large = max_exact + (jnp.log(ratio) / math.log(T5_MAX_DIST / max_exact) * (T5_BUCKETS - max_exact)).astype(i32)
        bucket = jnp.where(d < max_exact, d, jnp.minimum(large, T5_BUCKETS - 1))
        val = jnp.zeros((n, n), f32)
        for k in range(T5_BUCKETS):
            val = jnp.where(bucket == k, rb_ref[k, h], val)
        vals.append(val)
    far = vals[1][n - 1:n, 0:1]
    d_ref[0] = vals[0] - far
    p_ref[0] = vals[1] - far


def bias_tiles(rel_bias):
    n = MOBA_BLOCK
    return pl.pallas_call(
        _bias_tiles_body,
        grid=(N_HEADS,),
        in_specs=[pl.BlockSpec(memory_space=pltpu.SMEM)],
        out_specs=[pl.BlockSpec((1, n, n), lambda h: (h, 0, 0))] * 2,
        out_shape=[jax.ShapeDtypeStruct((N_HEADS, n, n), f32)] * 2,
        compiler_params=_cp(("arbitrary",)),
        name="bias_tiles",
    )(rel_bias)


def _in_proj_body(x_ref, g_ref, w_ref, ws_ref, gain_ref, bd_ref, o_ref, os_ref, o16_ref, h_scr):
    j = pl.program_id(1)

    @pl.when(j == 0)
    def _():
        x = x_ref[...]
        h = x * lax.rsqrt(jnp.mean(x * x, axis=-1, keepdims=True) + RMS_EPS) * g_ref[...]
        hb = h.astype(bf16)
        h_scr[...] = hb
        os_ref[...] = jnp.dot(hb, ws_ref[...], preferred_element_type=f32)

    acc = jnp.dot(h_scr[...], w_ref[...], preferred_element_type=f32)

    @pl.when(j < N_NORMED)
    def _():
        sq = acc * acc
        hi = sq.astype(bf16)
        lo = (sq - hi.astype(f32)).astype(bf16)
        ms = (jnp.dot(hi, bd_ref[...], preferred_element_type=f32)
              + jnp.dot(lo, bd_ref[...], preferred_element_type=f32))
        normed = acc * lax.rsqrt(ms + RMS_EPS) * gain_ref[0]
        o_ref[...] = normed
        o16_ref[...] = normed.astype(bf16)

    @pl.when(j >= N_NORMED)
    def _():
        o_ref[...] = acc

    @pl.when((j >= N_NORMED) & (j < N_ATT_TILES))
    def _():
        o16_ref[...] = acc.astype(bf16)


def in_proj(x, g_mix, w_main, w_small, gains, tm):
    t = x.shape[0]
    nj = C_END // PROJ_TN
    head_avg = jnp.kron(jnp.eye(N_HEADS, dtype=f32), jnp.full((HEAD_DIM, HEAD_DIM), 1.0 / HEAD_DIM, f32)).astype(bf16)
    return pl.pallas_call(
        _in_proj_body,
        grid=(t // tm, nj),
        in_specs=[
            pl.BlockSpec((tm, D_MODEL), lambda i, j: (i, 0)),
            pl.BlockSpec((1, D_MODEL), lambda i, j: (0, 0)),
            pl.BlockSpec((D_MODEL, PROJ_TN), lambda i, j: (0, j)),
            pl.BlockSpec((D_MODEL, S_W), lambda i, j: (0, 0)),
            pl.BlockSpec((1, 1, PROJ_TN), lambda i, j: (jnp.minimum(j, N_NORMED - 1), 0, 0)),
            pl.BlockSpec((PROJ_TN, PROJ_TN), lambda i, j: (0, 0)),
        ],
        out_specs=[
            pl.BlockSpec((tm, PROJ_TN), lambda i, j: (i, j)),
            pl.BlockSpec((tm, S_W), lambda i, j: (i, 0)),
            pl.BlockSpec((tm, PROJ_TN), lambda i, j: (i, jnp.minimum(j, N_ATT_TILES - 1))),
        ],
        out_shape=[jax.ShapeDtypeStruct((t, C_END), f32), jax.ShapeDtypeStruct((t, S_W), f32),
                   jax.ShapeDtypeStruct((t, N_ATT_TILES * PROJ_TN), bf16)],
        scratch_shapes=[pltpu.VMEM((tm, D_MODEL), bf16)],
        compiler_params=_cp(("arbitrary", "arbitrary")),
        name="in_proj",
    )(x, g_mix.reshape(1, D_MODEL), w_main, w_small, gains, head_avg)


def _logf_body(s_ref, b_ref, lf_ref, cum_ref, carry, *, chunks_per_seq):
    i = pl.program_id(0)
    n = s_ref.shape[0]

    @pl.when(i % chunks_per_seq == 0)
    def _():
        carry[...] = jnp.zeros_like(carry)

    ff = s_ref[:, S_FF:S_FF + N_HEADS] + b_ref[...]
    logf = -_softplus(-ff)
    lf_ref[...] = logf
    row = lax.broadcasted_iota(i32, (n, n), 0)
    col = lax.broadcasted_iota(i32, (n, n), 1)
    tri = (row >= col).astype(f32)
    c = jnp.dot(tri, logf, precision=HI, preferred_element_type=f32) + carry[...]
    cum_ref[...] = c
    carry[...] = c[n - 1:n, :]


def logf_cum(small, f_bias, seq_len, tc=256):
    t = small.shape[0]
    return pl.pallas_call(
        functools.partial(_logf_body, chunks_per_seq=seq_len // tc),
        grid=(t // tc,),
        in_specs=[pl.BlockSpec((tc, S_W), lambda i: (i, 0)), pl.BlockSpec((1, N_HEADS), lambda i: (0, 0))],
        out_specs=[pl.BlockSpec((tc, N_HEADS), lambda i: (i, 0))] * 2,
        out_shape=[jax.ShapeDtypeStruct((t, N_HEADS), f32)] * 2,
        scratch_shapes=[pltpu.VMEM((1, N_HEADS), f32)],
        compiler_params=_cp(("arbitrary",)),
        name="logf_cum",
    )(small, f_bias.reshape(1, N_HEADS))


def _half_masks():
    lane = lax.broadcasted_iota(i32, (1, 2 * HEAD_DIM), 1)
    return [lane < HEAD_DIM, lane >= HEAD_DIM]


def _softmax_step(s, m, l, acc, v):
    m_new = jnp.maximum(m, jnp.max(s, axis=-1, keepdims=True))
    alpha = jnp.exp(m - m_new)
    p = jnp.exp(s - m_new)
    l = l * alpha + jnp.sum(p, axis=-1, keepdims=True)
    acc = acc * alpha + jnp.dot(p.astype(bf16), v, preferred_element_type=f32)
    return m_new, l, acc


def _softmax_init(tq):
    return (jnp.full((tq, 1), -jnp.inf, f32), jnp.zeros((tq, 1), f32), jnp.zeros((tq, 2 * HEAD_DIM), f32))


def _pair_output(res, half):
    return jnp.where(half[0], res[0][2] / res[0][1], res[1][2] / res[1][1])


def _fox_prompt_body(q_ref, k_ref, v_ref, cq_ref, ck_ref, o_ref, *, tq):
    pair = pl.program_id(1)
    qi = pl.program_id(2)
    half = _half_masks()
    q = q_ref[...] * ATT_SCALE
    qh = [jnp.where(half[hh], q, 0.0).astype(bf16) for hh in (0, 1)]
    lane8 = lax.broadcasted_iota(i32, (1, N_HEADS), 1)
    sub8 = lax.broadcasted_iota(i32, (N_HEADS, 1), 0)
    cq_all = cq_ref[...]
    cq = [jnp.sum(jnp.where(lane8 == 2 * pair + hh, cq_all, 0.0), axis=1, keepdims=True) for hh in (0, 1)]

    def tile(kb, carry, causal):
        off = pl.multiple_of(kb * tq, tq)
        k = k_ref[pl.ds(off, tq), :]
        v = v_ref[pl.ds(off, tq), :]
        ck_all = ck_ref[0, :, pl.ds(off, tq)]
        out = []
        for hh in (0, 1):
            ck = jnp.sum(jnp.where(sub8 == 2 * pair + hh, ck_all, 0.0), axis=0, keepdims=True)
            s = lax.dot_general(qh[hh], k, NT, preferred_element_type=f32) + cq[hh] - ck
            if causal is not None:
                s = jnp.where(causal, s, -jnp.inf)
            out.append(_softmax_step(s, *carry[hh], v))
        return tuple(out)

    res = lax.fori_loop(0, qi, lambda kb, c: tile(kb, c, None), (_softmax_init(tq), _softmax_init(tq)))
    causal = lax.broadcasted_iota(i32, (tq, tq), 1) <= lax.broadcasted_iota(i32, (tq, tq), 0)
    res = tile(qi, res, causal)
    o_ref[...] = _pair_output(res, half)


def _rank_select(gates):
    out = []
    for n, gn in enumerate(gates):
        rank = jnp.zeros(gn.shape, i32)
        for m, gm in enumerate(gates):
            if m < n:
                rank = rank + (gm >= gn).astype(i32)
            elif m > n:
                rank = rank + (gm > gn).astype(i32)
        out.append(rank < MOBA_TOPK)
    return out


def _moba_prompt_body(q_ref, k_ref, v_ref, bd_ref, bp_ref, k32_ref, o_ref, kmean_scr, *, tq, nblk):
    qi = pl.program_id(2)
    half = _half_masks()

    @pl.when(qi == 0)
    def _():
        kmean_scr[...] = jnp.mean(k32_ref[...].reshape(nblk, tq, 2 * HEAD_DIM), axis=1)

    q = q_ref[...]
    qs = q * ATT_SCALE
    qh = [jnp.where(half[hh], qs, 0.0).astype(bf16) for hh in (0, 1)]
    kmean = kmean_scr[...]
    neg = jnp.float32(-jnp.inf)
    sel = []
    for hh in (0, 1):
        gate = lax.dot_general(jnp.where(half[hh], q, 0.0), kmean, NT, precision=HI, preferred_element_type=f32)
        cols = [jnp.where(n < qi, gate[:, n:n + 1], neg) for n in range(nblk)]
        picks = _rank_select(cols)
        sel.append([jnp.where(picks[n] & (n < qi), 0.0, neg) for n in range(nblk)])

    def tile(kb, carry, bias_ref, causal, masks):
        off = pl.multiple_of(kb * tq, tq)
        k = k_ref[pl.ds(off, tq), :]
        v = v_ref[pl.ds(off, tq), :]
        out = []
        for hh in (0, 1):
            s = lax.dot_general(qh[hh], k, NT, preferred_element_type=f32)
            if bias_ref is not None:
                s = s + bias_ref[hh]
            if causal is not None:
                s = jnp.where(causal, s, neg)
            if masks is not None:
                s = s + masks[hh]
            out.append(_softmax_step(s, *carry[hh], v))
        return tuple(out)

    def row_mask(kb):
        out = []
        for hh in (0, 1):
            mk = sel[hh][0]
            for n in range(1, nblk):
                mk = jnp.where(kb == n, sel[hh][n], mk)
            out.append(mk)
        return out

    causal = lax.broadcasted_iota(i32, (tq, tq), 1) <= lax.broadcasted_iota(i32, (tq, tq), 0)
    res = tile(qi, (_softmax_init(tq), _softmax_init(tq)), bd_ref, causal, None)
    prev = jnp.maximum(qi - 1, 0)
    res = tile(prev, res, bp_ref, None, row_mask(prev))
    res = lax.fori_loop(0, qi - 1, lambda kb, c: tile(kb, c, None, None, row_mask(kb)), res)
    o_ref[...] = _pair_output(res, half)


def attn_prompt(mode, proj, qkv16, batch, seq, c_q, c_k, c_v, extra, tq=MOBA_BLOCK):
    nq = seq // tq
    npair = N_HEADS // 2
    pw = 2 * HEAD_DIM
    in_specs = [
        pl.BlockSpec((tq, pw), lambda b, p, i: (b * nq + i, c_q // pw + p)),
        pl.BlockSpec((seq, pw), lambda b, p, i: (b, c_k // pw + p)),
        pl.BlockSpec((seq, pw), lambda b, p, i: (b, c_v // pw + p)),
    ]
    if mode == "fox":
        cum, cum_t = extra
        body = functools.partial(_fox_prompt_body, tq=tq)
        in_specs += [pl.BlockSpec((tq, N_HEADS), lambda b, p, i: (b * nq + i, 0)),
                     pl.BlockSpec((1, N_HEADS, seq), lambda b, p, i: (b, 0, 0))]
        args = (cum, cum_t)
        scratch = []
    else:
        bias_d, bias_p = extra
        body = functools.partial(_moba_prompt_body, tq=tq, nblk=nq)
        in_specs += [pl.BlockSpec((2, tq, tq), lambda b, p, i: (p, 0, 0))] * 2 + [in_specs[1]]
        args = (bias_d, bias_p, proj)
        scratch = [pltpu.VMEM((nq, pw), f32)]
    return pl.pallas_call(
        body,
        grid=(batch, npair, nq),
        in_specs=in_specs,
        out_specs=pl.BlockSpec((tq, pw), lambda b, p, i: (b * nq + i, p)),
        out_shape=jax.ShapeDtypeStruct((batch * seq, ATT_W), f32),
        scratch_shapes=scratch,
        compiler_params=_cp(("arbitrary", "arbitrary", "arbitrary")),
        name=f"{mode}_prompt",
    )(proj, qkv16, qkv16, *args)


def _expand_q(q):
    nq = q.shape[0]
    rows = N_HEADS * nq
    qt = jnp.broadcast_to(q[None], (N_HEADS, nq, ATT_W)).reshape(rows, ATT_W)
    rh = lax.broadcasted_iota(i32, (rows, ATT_W), 0) // nq
    lh = lax.broadcasted_iota(i32, (rows, ATT_W), 1) // HEAD_DIM
    return qt, rh == lh


def _rows_from_heads(x_t, nq):
    return jnp.broadcast_to(x_t[:, None, :], (N_HEADS, nq, x_t.shape[1])).reshape(N_HEADS * nq, x_t.shape[1])


def _cum_lanes(lf, carry):
    n = lf.shape[1]
    upper = (lax.broadcasted_iota(i32, (n, n), 0) <= lax.broadcasted_iota(i32, (n, n), 1)).astype(bf16)
    hi = lf.astype(bf16).astype(f32)
    mid = (lf - hi).astype(bf16).astype(f32)
    lo = lf - hi - mid
    parts = jnp.dot(jnp.concatenate([hi, mid, lo], axis=0).astype(bf16), upper, preferred_element_type=f32)
    h = lf.shape[0]
    c = parts[0:h] + parts[h:2 * h] + parts[2 * h:3 * h] + carry
    return c, c[:, n - 1:n]


def _sample_attn_body(pt_ref, q_ref, kn_ref, vn_ref, *refs, mode, npages, nq):
    del pt_ref
    if mode == "fox":
        lfn_ref, refs = refs[0], refs[1:]
    else:
        bo_ref, bl_ref, refs = refs[0], refs[1], refs[2:]
    kp, vp, refs = refs[:npages], refs[npages:2 * npages], refs[2 * npages:]
    if mode == "fox":
        lp, refs = refs[:npages], refs[npages:]
    o_ref = refs[0]
    rows = N_HEADS * nq
    neg = jnp.float32(-jnp.inf)

    q = q_ref[...]
    qt, diag = _expand_q(q)
    qe = jnp.where(diag, qt * ATT_SCALE, 0.0).astype(bf16)
    s_past = [jnp.dot(qe, kp[j][0].astype(bf16), preferred_element_type=f32) for j in range(npages)]
    s_new = lax.dot_general(qe, kn_ref[...].astype(bf16), NT, preferred_element_type=f32)
    qpos = lax.broadcasted_iota(i32, (rows, nq), 0) % nq
    kpos = lax.broadcasted_iota(i32, (rows, nq), 1)

    if mode == "fox":
        carry = jnp.zeros((N_HEADS, 1), f32)
        for j in range(npages):
            c, carry = _cum_lanes(lp[j][0], carry)
            s_past[j] = s_past[j] - _rows_from_heads(c, nq)
        c, _ = _cum_lanes(lfn_ref[...].T, carry)
        s_new = s_new - _rows_from_heads(c, nq)
    else:
        ppb = MOBA_BLOCK // PAGE
        nblk = npages // ppb
        q_t = q.T
        gates = []
        for n in range(nblk):
            ksum = kp[ppb * n][0]
            for j in range(1, ppb):
                ksum = ksum + kp[ppb * n + j][0]
            kmean = jnp.sum(ksum, axis=1, keepdims=True) * (1.0 / MOBA_BLOCK)
            gates.append(jnp.sum((q_t * kmean).reshape(N_HEADS, HEAD_DIM, nq), axis=1))
        picks = _rank_select(gates)
        own_lane = kpos == qpos
        for n in range(nblk):
            spread = _rows_from_heads(jnp.where(picks[n], 0.0, neg), nq)
            mask = jnp.min(jnp.where(own_lane, spread, 0.0), axis=1, keepdims=True)
            for j in range(ppb):
                pg = ppb * n + j
                s = s_past[pg] + mask
                if n == nblk - 1:
                    s = s + bl_ref[:, j * PAGE:(j + 1) * PAGE]
                s_past[pg] = s
        s_new = s_new + bo_ref[...]

    s_new = jnp.where(kpos <= qpos, s_new, neg)
    m = jnp.max(s_new, axis=-1, keepdims=True)
    for j in range(npages):
        m = jnp.maximum(m, jnp.max(s_past[j], axis=-1, keepdims=True))
    p = jnp.exp(s_new - m)
    l = jnp.sum(p, axis=-1, keepdims=True)
    acc = jnp.dot(p.astype(bf16), vn_ref[...].astype(bf16), preferred_element_type=f32)
    for j in range(npages):
        p = jnp.exp(s_past[j] - m)
        l = l + jnp.sum(p, axis=-1, keepdims=True)
        acc = acc + lax.dot_general(p.astype(bf16), vp[j][0].astype(bf16), NT, preferred_element_type=f32)
    out = jnp.where(diag, acc / l, 0.0)
    o_ref[...] = jnp.sum(out.reshape(N_HEADS, nq, ATT_W), axis=0)


def attn_sample(mode, proj, row0, n_seq, nq, c_q, c_k, c_v, pools, page_table, layer, extra):
    k_pool, v_pool = pools
    n_pool = k_pool.shape[1]
    npages = page_table.shape[1]
    assert (npages * PAGE) % MOBA_BLOCK == 0 and nq <= MOBA_BLOCK
    k_t = k_pool.transpose(0, 1, 3, 4, 2).reshape(-1, ATT_W, PAGE)
    v_t = v_pool.transpose(0, 1, 3, 4, 2).reshape(-1, ATT_W, PAGE)
    rb0 = row0 // nq

    def new_spec(c0):
        return pl.BlockSpec((nq, ATT_W), lambda b, pt: (rb0 + b, c0 // ATT_W))

    def page_spec(j, height):
        return pl.BlockSpec((1, height, PAGE), lambda b, pt: (layer * n_pool + pt[b, j], 0, 0))

    rows = N_HEADS * nq
    kv_specs = [page_spec(j, ATT_W) for j in range(npages)] * 2
    if mode == "fox":
        logf_new, lf_pool = extra
        lf_t = lf_pool.transpose(0, 1, 3, 2).reshape(-1, N_HEADS, PAGE)
        in_specs = ([new_spec(c_q), new_spec(c_k), new_spec(c_v), pl.BlockSpec((nq, N_HEADS), lambda b, pt: (rb0 + b, 0))]
                    + kv_specs + [page_spec(j, N_HEADS) for j in range(npages)])
        args = (proj, proj, proj, logf_new) + (k_t,) * npages + (v_t,) * npages + (lf_t,) * npages
    else:
        bias_own, bias_last = extra
        in_specs = ([new_spec(c_q), new_spec(c_k), new_spec(c_v),
                     pl.BlockSpec((rows, nq), lambda b, pt: (0, 0)), pl.BlockSpec((rows, MOBA_BLOCK), lambda b, pt: (0, 0))]
                    + kv_specs)
        args = (proj, proj, proj, bias_own, bias_last) + (k_t,) * npages + (v_t,) * npages
    return pl.pallas_call(
        functools.partial(_sample_attn_body, mode=mode, npages=npages, nq=nq),
        grid_spec=pltpu.PrefetchScalarGridSpec(
            num_scalar_prefetch=1,
            grid=(n_seq,),
            in_specs=in_specs,
            out_specs=pl.BlockSpec((nq, ATT_W), lambda b, pt: (b, 0)),
        ),
        out_shape=jax.ShapeDtypeStruct((n_seq * nq, ATT_W), f32),
        compiler_params=_cp(("arbitrary",)),
        name=f"{mode}_sample",
    )(page_table, *args)


def _ssd_body(z_ref, xs_ref, bc_ref, sm_ref, cs_ref, h0_ref, cw_ref, cb_ref, dtb_ref, alog_ref, dsk_ref, ng_ref,
              y_ref, conv_ref, ssm_ref, xp_scr, ht_scr, yd_scr, *, cl):
    c = pl.program_id(1)
    nc = pl.num_programs(1)
    tail = SSM_CONV - 1
    base = 8
    nh, hp, ns = SSM_HEADS, SSM_P, SSM_N
    gw = SSM_INNER // SSM_GROUPS

    @pl.when(c == 0)
    def _():
        xp_scr[base - tail:base, :] = cs_ref[0]
        ht_scr[...] = h0_ref[0].T

    xp_scr[base:base + cl, 0:SSM_INNER] = xs_ref[...]
    xp_scr[base:base + cl, SSM_INNER:CONV_DIM] = bc_ref[...]
    conv = cb_ref[...]
    for w in range(SSM_CONV):
        conv = conv + xp_scr[pl.ds(base - tail + w, cl), :] * cw_ref[w:w + 1, :]
    new_tail = xp_scr[base + cl - tail:base + cl, :]
    xp_scr[base - tail:base, :] = new_tail
    u = _silu(conv)
    xs = u[:, :SSM_INNER]
    bm = [u[:, SSM_INNER + g * ns:SSM_INNER + (g + 1) * ns] for g in range(SSM_GROUPS)]
    cm = [u[:, SSM_INNER + (SSM_GROUPS + g) * ns:SSM_INNER + (SSM_GROUPS + g + 1) * ns] for g in range(SSM_GROUPS)]

    dt = _softplus(sm_ref[:, S_DT:S_DT + nh] + dtb_ref[...])
    a = dt * (-jnp.exp(alog_ref[...]))
    row = lax.broadcasted_iota(i32, (cl, cl), 0)
    col = lax.broadcasted_iota(i32, (cl, cl), 1)
    causal = row >= col
    acum = jnp.dot(causal.astype(f32), a, precision=HI, preferred_element_type=f32)
    eye = (lax.broadcasted_iota(i32, (nh, nh), 0) == lax.broadcasted_iota(i32, (nh, nh), 1)).astype(f32)
    acum_t = lax.dot_general(eye, acum, NT, precision=HI, preferred_element_type=f32)
    a_end = acum[cl - 1:cl, :]
    spread = (lax.broadcasted_iota(i32, (nh, SSM_INNER), 1) // hp == lax.broadcasted_iota(i32, (nh, SSM_INNER), 0)).astype(f32)

    def lanes(x):
        return jnp.dot(x, spread, precision=HI, preferred_element_type=f32)

    xdt = xs * lanes(dt)
    half = _half_masks()

    cb = [lax.dot_general(cm[g].astype(bf16), bm[g].astype(bf16), NT, preferred_element_type=f32) for g in range(SSM_GROUPS)]
    for pr in range(nh // 2):
        xp_pair = xdt[:, pr * 2 * hp:(pr + 1) * 2 * hp]
        acc = None
        for hh in (0, 1):
            hd = 2 * pr + hh
            seg = acum[:, hd:hd + 1] - acum_t[hd:hd + 1, :]
            decay = jnp.exp(jnp.where(causal, seg, -jnp.inf))
            mm = (cb[hd // (nh // SSM_GROUPS)] * decay).astype(bf16)
            t = jnp.dot(mm, jnp.where(half[hh], xp_pair, 0.0).astype(bf16), preferred_element_type=f32)
            acc = t if acc is None else acc + t
        yd_scr[:, pr * 2 * hp:(pr + 1) * 2 * hp] = acc

    ht = ht_scr[...]
    xw = (xdt * lanes(jnp.exp(a_end - acum))).astype(bf16)
    y_off = []
    st = []
    for g in range(SSM_GROUPS):
        y_off.append(jnp.dot(cm[g].astype(bf16), ht[:, g * gw:(g + 1) * gw].astype(bf16), preferred_element_type=f32))
        st.append(lax.dot_general(bm[g].astype(bf16), xw[:, g * gw:(g + 1) * gw], TN, preferred_element_type=f32))
    from_start = lanes(jnp.exp(acum))
    y = yd_scr[...] + jnp.concatenate(y_off, axis=1) * from_start + xs * dsk_ref[...]
    ht_new = ht * from_start[cl - 1:cl, :] + jnp.concatenate(st, axis=1)
    ht_scr[...] = ht_new

    yg = y * _silu(z_ref[...])
    outs = []
    for g in range(SSM_GROUPS):
        part = yg[:, g * gw:(g + 1) * gw]
        outs.append(part * lax.rsqrt(jnp.mean(part * part, axis=-1, keepdims=True) + RMS_EPS))
    y_ref[...] = jnp.concatenate(outs, axis=1) * ng_ref[...]

    @pl.when(c == nc - 1)
    def _():
        conv_ref[0] = new_tail
        ssm_ref[0] = ht_new.T


def ssd(proj, small, row0, n_seq, seq, cl, conv_state, ssm_state, conv_w, conv_b, dt_bias, a_log, d_skip, norm_g):
    nc = seq // cl
    rb0 = row0 // cl
    h0 = ssm_state.reshape(n_seq, SSM_HEADS * SSM_P, SSM_N)

    def rows(width, c0):
        return pl.BlockSpec((cl, width), lambda b, c: (rb0 + b * nc + c, c0 // width))

    def const(shape):
        return pl.BlockSpec(shape, lambda b, c: (0,) * len(shape))

    y, new_conv, new_ssm = pl.pallas_call(
        functools.partial(_ssd_body, cl=cl),
        grid=(n_seq, nc),
        in_specs=[
            rows(SSM_INNER, C_Z), rows(SSM_INNER, C_XS), rows(CONV_DIM - SSM_INNER, C_BC), rows(S_W, 0),
            pl.BlockSpec((1, SSM_CONV - 1, CONV_DIM), lambda b, c: (b, 0, 0)),
            pl.BlockSpec((1, SSM_HEADS * SSM_P, SSM_N), lambda b, c: (b, 0, 0)),
            const((SSM_CONV, CONV_DIM)), const((1, CONV_DIM)), const((1, SSM_HEADS)), const((1, SSM_HEADS)),
            const((1, SSM_INNER)), const((1, SSM_INNER)),
        ],
        out_specs=[
            pl.BlockSpec((cl, SSM_INNER), lambda b, c: (b * nc + c, 0)),
            pl.BlockSpec((1, SSM_CONV - 1, CONV_DIM), lambda b, c: (b, 0, 0)),
            pl.BlockSpec((1, SSM_HEADS * SSM_P, SSM_N), lambda b, c: (b, 0, 0)),
        ],
        out_shape=[
            jax.ShapeDtypeStruct((n_seq * seq, SSM_INNER), f32),
            jax.ShapeDtypeStruct((n_seq, SSM_CONV - 1, CONV_DIM), f32),
            jax.ShapeDtypeStruct((n_seq, SSM_HEADS * SSM_P, SSM_N), f32),
        ],
        scratch_shapes=[pltpu.VMEM((8 + cl, CONV_DIM), f32), pltpu.VMEM((SSM_N, SSM_INNER), f32),
                        pltpu.VMEM((cl, SSM_INNER), f32)],
        compiler_params=_cp(("arbitrary", "arbitrary")),
        name=f"ssd_cl{cl}",
    )(proj, proj, proj, small, conv_state, h0, conv_w, conv_b.reshape(1, CONV_DIM), dt_bias.reshape(1, SSM_HEADS),
      a_log.reshape(1, SSM_HEADS), jnp.repeat(d_skip, SSM_P).reshape(1, SSM_INNER), norm_g.reshape(1, SSM_INNER))
    return y, new_conv, new_ssm.reshape(n_seq, SSM_HEADS, SSM_P, SSM_N)


def _merge_body(x_ref, ya_ref, ym_ref, yc_ref, ga_ref, gb_ref, gc_ref, wa_ref, wb_ref, wc_ref, wo_ref, o_ref):
    def branch(y_ref, w_ref, g_ref):
        return jax.nn.sigmoid(g_ref[...]) * jnp.dot(y_ref[...].astype(bf16), w_ref[...], preferred_element_type=f32)

    merged = branch(ya_ref, wa_ref, ga_ref) + branch(ym_ref, wb_ref, gb_ref) + branch(yc_ref, wc_ref, gc_ref)
    o_ref[...] = x_ref[...] + jnp.dot(merged.astype(bf16), wo_ref[...], preferred_element_type=f32)


def merge(x, ya, ym, yc, proj, wa, wb, wc, wo, tm):
    t = x.shape[0]
    g0 = C_GL // D_MODEL

    def rows(width, cb=0):
        return pl.BlockSpec((tm, width), lambda i: (i, cb))

    def const(shape):
        return pl.BlockSpec(shape, lambda i: (0, 0))

    return pl.pallas_call(
        _merge_body,
        grid=(t // tm,),
        in_specs=[rows(D_MODEL), rows(ATT_W), rows(ATT_W), rows(SSM_INNER),
                  rows(D_MODEL, g0), rows(D_MODEL, g0 + 1), rows(D_MODEL, g0 + 2),
                  const((ATT_W, D_MODEL)), const((ATT_W, D_MODEL)), const((SSM_INNER, D_MODEL)), const((D_MODEL, D_MODEL))],
        out_specs=rows(D_MODEL),
        out_shape=jax.ShapeDtypeStruct((t, D_MODEL), f32),
        compiler_params=_cp(("arbitrary",)),
        name="merge",
    )(x, ya, ym, yc, proj, proj, proj, wa, wb, wc, wo)


def _router_body(x_ref, g_ref, wr_ref, br_ref, h_ref, eid_ref, rank_ref, ewt_ref, cnt_ref, cnt_scr, *, tm):
    i = pl.program_id(0)

    @pl.when(i == 0)
    def _():
        cnt_scr[...] = jnp.zeros_like(cnt_scr)

    x = x_ref[...]
    h = x * lax.rsqrt(jnp.mean(x * x, axis=-1, keepdims=True) + RMS_EPS) * g_ref[...]
    h_ref[...] = h
    logits = lax.dot_general(wr_ref[...], h, NT, precision=HI, preferred_element_type=f32) + br_ref[...]
    sub = lax.broadcasted_iota(i32, (8, tm), 0)
    gl = jnp.where(sub < N_GROUPS_E, logits[0:8], -jnp.inf)
    gmax = jnp.max(gl, axis=0, keepdims=True)
    g_top = 1.0 / jnp.sum(jnp.exp(gl - gmax), axis=0, keepdims=True)
    g_idx = jnp.min(jnp.where(gl == gmax, sub, 8), axis=0, keepdims=True)
    e_in = jnp.zeros((E_PER_GROUP, tm), f32)
    for g in range(N_GROUPS_E):
        e_in = jnp.where(g_idx == g, logits[8 + g * E_PER_GROUP:8 + (g + 1) * E_PER_GROUP], e_in)
    ex = jnp.exp(e_in - jnp.max(e_in, axis=0, keepdims=True))
    prob = ex / jnp.sum(ex, axis=0, keepdims=True)
    p1 = jnp.max(prob, axis=0, keepdims=True)
    i1 = jnp.min(jnp.where(prob == p1, sub, 8), axis=0, keepdims=True)
    rest = jnp.where(sub == i1, -1.0, prob)
    p2 = jnp.max(rest, axis=0, keepdims=True)
    i2 = jnp.min(jnp.where(rest == p2, sub, 8), axis=0, keepdims=True)
    denom = p1 + p2
    ids = [g_idx * E_PER_GROUP + i1, g_idx * E_PER_GROUP + i2]
    wts = [g_top * p1 / denom, g_top * p2 / denom]
    eid_ref[...] = jnp.concatenate(ids, axis=0)
    ewt_ref[...] = jnp.concatenate(wts + [jnp.zeros((6, tm), f32)], axis=0).T

    esub = lax.broadcasted_iota(i32, (N_EXPERTS, tm), 0)
    oh = [(esub == ids[k]).astype(f32) for k in (0, 1)]
    both = oh[0] + oh[1]
    before = (lax.broadcasted_iota(i32, (tm, tm), 0) < lax.broadcasted_iota(i32, (tm, tm), 1)).astype(bf16)
    pos = jnp.dot(both.astype(bf16), before, preferred_element_type=f32) + cnt_scr[:, 0:1]
    rank_ref[...] = jnp.concatenate([jnp.sum(oh[k] * pos, axis=0, keepdims=True) for k in (0, 1)], axis=0).astype(i32)
    cnt_scr[...] = cnt_scr[...] + jnp.sum(both, axis=1, keepdims=True)
    cnt_ref[...] = cnt_scr[...].astype(i32)


def router(x, g_ffn, w_router_t, b_router, tm):
    t = x.shape[0]
    nr = w_router_t.shape[0]
    return pl.pallas_call(
        functools.partial(_router_body, tm=tm),
        grid=(t // tm,),
        in_specs=[pl.BlockSpec((tm, D_MODEL), lambda i: (i, 0)), pl.BlockSpec((1, D_MODEL), lambda i: (0, 0)),
                  pl.BlockSpec((nr, D_MODEL), lambda i: (0, 0)), pl.BlockSpec((nr, 1), lambda i: (0, 0))],
        out_specs=[pl.BlockSpec((tm, D_MODEL), lambda i: (i, 0)),
                   pl.BlockSpec((2, tm), lambda i: (0, i)), pl.BlockSpec((2, tm), lambda i: (0, i)),
                   pl.BlockSpec((tm, 8), lambda i: (i, 0)),
                   pl.BlockSpec((N_EXPERTS, 128), lambda i: (0, 0))],
        out_shape=[jax.ShapeDtypeStruct((t, D_MODEL), f32),
                   jax.ShapeDtypeStruct((2, t), i32), jax.ShapeDtypeStruct((2, t), i32),
                   jax.ShapeDtypeStruct((t, 8), f32),
                   jax.ShapeDtypeStruct((N_EXPERTS, 128), i32)],
        scratch_shapes=[pltpu.VMEM((N_EXPERTS, 128), f32)],
        compiler_params=_cp(("arbitrary",)),
        name="router",
    )(x, g_ffn.reshape(1, D_MODEL), w_router_t, b_router)


def _dispatch_body(pstart_ref, eid_ref, rank_ref, h_ref, xs_in_ref, xs_ref, dest_ref, sem, *, tm):
    del xs_in_ref

    def copy(t, d):
        return pltpu.make_async_copy(h_ref.at[pl.ds(t, 1), :], xs_ref.at[pl.ds(d, 1), :], sem)

    def issue(t, _):
        for k in (0, 1):
            d = pstart_ref[eid_ref[k, t]] + rank_ref[k, t]
            dest_ref[k, t] = d
            copy(t, d).start()
        return 0

    lax.fori_loop(0, tm, issue, 0)

    def drain(t, _):
        for k in (0, 1):
            copy(t, dest_ref[k, t]).wait()
        return 0

    lax.fori_loop(0, tm, drain, 0)


def dispatch(h, eid, rank, pstart, cap, tm):
    t = h.shape[0]
    smem_rows = pl.BlockSpec((2, tm), lambda i, ps: (0, i), memory_space=pltpu.SMEM)
    return pl.pallas_call(
        functools.partial(_dispatch_body, tm=tm),
        grid_spec=pltpu.PrefetchScalarGridSpec(
            num_scalar_prefetch=1,
            grid=(t // tm,),
            in_specs=[smem_rows, smem_rows, pl.BlockSpec((tm, D_MODEL), lambda i, ps: (i, 0)),
                      pl.BlockSpec(memory_space=pl.ANY)],
            out_specs=[pl.BlockSpec(memory_space=pl.ANY), smem_rows],
            scratch_shapes=[pltpu.SemaphoreType.DMA(())],
        ),
        out_shape=[jax.ShapeDtypeStruct((cap, D_MODEL), f32), jax.ShapeDtypeStruct((2, t), i32)],
        input_output_aliases={4: 0},
        compiler_params=_cp(("arbitrary",)),
        name="moe_dispatch",
    )(pstart, eid, rank, h, jnp.zeros((cap, D_MODEL), f32))


def _experts_body(be_ref, x_ref, w1_ref, w3_ref, w2_ref, o_ref):
    xb = x_ref[...].astype(bf16)
    a = jnp.dot(xb, w1_ref[0], preferred_element_type=f32)
    b = jnp.dot(xb, w3_ref[0], preferred_element_type=f32)
    o_ref[...] = jnp.dot((_silu(a) * b).astype(bf16), w2_ref[0], preferred_element_type=f32)


def experts(xs, blk_expert, w1, w3, w2, blk):
    cap = xs.shape[0]
    return pl.pallas_call(
        _experts_body,
        grid_spec=pltpu.PrefetchScalarGridSpec(
            num_scalar_prefetch=1,
            grid=(cap // blk,),
            in_specs=[pl.BlockSpec((blk, D_MODEL), lambda i, be: (i, 0)),
                      pl.BlockSpec((1, D_MODEL, D_EXPERT), lambda i, be: (be[i], 0, 0)),
                      pl.BlockSpec((1, D_MODEL, D_EXPERT), lambda i, be: (be[i], 0, 0)),
                      pl.BlockSpec((1, D_EXPERT, D_MODEL), lambda i, be: (be[i], 0, 0))],
            out_specs=pl.BlockSpec((blk, D_MODEL), lambda i, be: (i, 0)),
        ),
        out_shape=jax.ShapeDtypeStruct((cap, D_MODEL), f32),
        compiler_params=_cp(("arbitrary",)),
        name="moe_experts",
    )(blk_expert, xs, w1, w3, w2)


def _combine_body(dest_ref, x_ref, ewt_ref, ys_ref, o_ref, buf, sem, *, tm):
    def copy(t, k):
        return pltpu.make_async_copy(ys_ref.at[pl.ds(dest_ref[k, t], 1), :], buf.at[k, pl.ds(t, 1), :], sem)

    def issue(t, _):
        for k in (0, 1):
            copy(t, k).start()
        return 0

    lax.fori_loop(0, tm, issue, 0)

    def drain(t, _):
        for k in (0, 1):
            copy(t, k).wait()
        return 0

    lax.fori_loop(0, tm, drain, 0)
    w = ewt_ref[...]
    o_ref[...] = x_ref[...] + (buf[0] * w[:, 0:1] + buf[1] * w[:, 1:2])


def combine(x, ewt, dest, ys, tm):
    t = x.shape[0]
    return pl.pallas_call(
        functools.partial(_combine_body, tm=tm),
        grid=(t // tm,),
        in_specs=[pl.BlockSpec((2, tm), lambda i: (0, i), memory_space=pltpu.SMEM),
                  pl.BlockSpec((tm, D_MODEL), lambda i: (i, 0)), pl.BlockSpec((tm, 8), lambda i: (i, 0)),
                  pl.BlockSpec(memory_space=pl.ANY)],
        out_specs=pl.BlockSpec((tm, D_MODEL), lambda i: (i, 0)),
        out_shape=jax.ShapeDtypeStruct((t, D_MODEL), f32),
        scratch_shapes=[pltpu.VMEM((2, tm, D_MODEL), f32), pltpu.SemaphoreType.DMA(())],
        compiler_params=_cp(("arbitrary",)),
        name="moe_combine",
    )(dest, x, ewt, ys)


def moe(x, g_ffn, w_router_t, b_router, w1, w3, w2, tm_router, tm_rows, blk):
    t = x.shape[0]
    h, eid, rank, ewt, counts = router(x, g_ffn, w_router_t, b_router, tm_router)
    counts = counts[:, 0]
    padded = (counts + blk - 1) // blk * blk
    pend = jnp.cumsum(padded)
    pstart = (pend - padded).astype(i32)
    nblocks = -(-2 * t // blk) + N_EXPERTS
    blk_start = jnp.arange(nblocks, dtype=i32) * blk
    blk_expert = jnp.minimum(jnp.sum((pend[None, :] <= blk_start[:, None]).astype(i32), axis=1), N_EXPERTS - 1)
    xs, dest = dispatch(h, eid, rank, pstart, nblocks * blk, tm_rows)
    ys = experts(xs, blk_expert, w1, w3, w2, blk)
    return combine(x, ewt, dest, ys, tm_rows)


def _prep_layer(l, w_in, fox_q_gain, fox_k_gain, moba_q_gain, moba_k_gain, router_group_w, router_group_b,
                router_expert_w, router_expert_b):
    w = w_in[l]
    o = [0]
    for s in (ATT_W, ATT_W, ATT_W, N_HEADS, ATT_W, ATT_W, ATT_W, SSM_INNER, CONV_DIM, SSM_HEADS, 3 * D_MODEL):
        o.append(o[-1] + s)
    fq, fk, fv, ff, mq, mk, mv, z, xbc, dtr, gl = [w[:, o[i]:o[i + 1]] for i in range(11)]
    w_main = jnp.concatenate([fq, fk, mq, mk, fv, mv, z, xbc[:, :SSM_INNER], gl, xbc[:, SSM_INNER:]], axis=1).astype(bf16)
    w_small = jnp.concatenate([dtr, ff, jnp.zeros((D_MODEL, S_W - SSM_HEADS - N_HEADS), f32)], axis=1).astype(bf16)
    gains = jnp.stack([jnp.tile(g[l], N_HEADS) for g in (fox_q_gain, fox_k_gain, moba_q_gain, moba_k_gain)]).reshape(4, 1, ATT_W)
    w_router_t = jnp.concatenate([router_group_w[l].T, jnp.zeros((8 - N_GROUPS_E, D_MODEL), f32), router_expert_w[l].T], axis=0)
    b_router = jnp.concatenate([router_group_b[l], jnp.zeros((8 - N_GROUPS_E,), f32), router_expert_b[l]]).reshape(-1, 1)
    return w_main, w_small, gains, w_router_t, b_router


def kernel(x_prompt, x_sample, cache_fox_k, cache_fox_v, cache_fox_logf, cache_moba_k, cache_moba_v, state_conv, state_ssm, page_table, rel_bias, g_mix, w_in, fox_q_gain, fox_k_gain, fox_f_bias, moba_q_gain, moba_k_gain, conv_w, conv_b, dt_bias, a_log, d_skip, ssm_norm_g, w_out_fox, w_out_moba, w_out_ssm, w_o, g_ffn, router_group_w, router_group_b, router_expert_w, router_expert_b, expert_w1, expert_w3, expert_w2):
    bp, lp, _ = x_prompt.shape
    bs, ls, _ = x_sample.shape
    tp, ts = bp * lp, bs * ls
    depth = w_in.shape[0]
    x = jnp.concatenate([x_prompt.reshape(tp, D_MODEL), x_sample.reshape(ts, D_MODEL)], axis=0)

    bias_d, bias_p = bias_tiles(rel_bias)
    rows = N_HEADS * ls
    bias_own = bias_d[:, :ls, :ls].reshape(rows, ls)
    bias_last = bias_p[:, :ls, :].reshape(rows, MOBA_BLOCK)
    zero_conv = jnp.zeros((bp, SSM_CONV - 1, CONV_DIM), f32)
    zero_ssm = jnp.zeros((bp, SSM_HEADS, SSM_P, SSM_N), f32)

    new_p = [[] for _ in range(7)]
    new_s = [[] for _ in range(7)]
    for l in range(depth):
        w_main, w_small, gains, w_router_t, b_router = _prep_layer(
            l, w_in, fox_q_gain, fox_k_gain, moba_q_gain, moba_k_gain, router_group_w, router_group_b,
            router_expert_w, router_expert_b)
        proj, small, qkv16 = in_proj(x, g_mix[l], w_main, w_small, gains, tm=1088)
        logf, cum = logf_cum(small, fox_f_bias[l], lp)
        cum_t = cum[:tp].reshape(bp, lp, N_HEADS).transpose(0, 2, 1)

        ya_p = attn_prompt("fox", proj, qkv16, bp, lp, C_FQ, C_FK, C_FV, (cum, cum_t))
        ym_p = attn_prompt("moba", proj, qkv16, bp, lp, C_MQ, C_MK, C_MV, (bias_d, bias_p))
        ya_s = attn_sample("fox", proj, tp, bs, ls, C_FQ, C_FK, C_FV, (cache_fox_k, cache_fox_v), page_table, l,
                           (logf, cache_fox_logf))
        ym_s = attn_sample("moba", proj, tp, bs, ls, C_MQ, C_MK, C_MV, (cache_moba_k, cache_moba_v), page_table, l,
                           (bias_own, bias_last))
        ssd_w = (conv_w[l], conv_b[l], dt_bias[l], a_log[l], d_skip[l], ssm_norm_g[l])
        yc_p, conv_p, ssm_p = ssd(proj, small, 0, bp, lp, math.gcd(lp, 128), zero_conv, zero_ssm, *ssd_w)
        yc_s, conv_s, ssm_s = ssd(proj, small, tp, bs, ls, math.gcd(ls, 128), state_conv[l], state_ssm[l], *ssd_w)

        x = merge(x, jnp.concatenate([ya_p, ya_s]), jnp.concatenate([ym_p, ym_s]), jnp.concatenate([yc_p, yc_s]), proj,
                  w_out_fox[l].astype(bf16), w_out_moba[l].astype(bf16), w_out_ssm[l].astype(bf16), w_o[l].astype(bf16),
                  tm=256)
        x = moe(x, g_ffn[l], w_router_t, b_router, expert_w1[l].astype(bf16), expert_w3[l].astype(bf16),
                expert_w2[l].astype(bf16), tm_router=512, tm_rows=256, blk=256)

        def heads(c0, r0, b, s):
            return proj[r0:r0 + b * s, c0:c0 + ATT_W].reshape(b, s, N_HEADS, HEAD_DIM)

        for lst, r0, b, s, conv_n, ssm_n in ((new_p, 0, bp, lp, conv_p, ssm_p), (new_s, tp, bs, ls, conv_s, ssm_s)):
            lst[0].append(heads(C_FK, r0, b, s))
            lst[1].append(heads(C_FV, r0, b, s))
            lst[2].append(logf[r0:r0 + b * s].reshape(b, s, N_HEADS))
            lst[3].append(heads(C_MK, r0, b, s))
            lst[4].append(heads(C_MV, r0, b, s))
            lst[5].append(conv_n)
            lst[6].append(ssm_n)

    yp = x[:tp].reshape(bp, lp, D_MODEL)
    ys = x[tp:].reshape(bs, ls, D_MODEL)
    return (yp, ys) + tuple(jnp.stack(a) for a in new_p) + tuple(jnp.stack(a) for a in new_s)
```

```python
import functools
import math

import jax
import jax.numpy as jnp
from jax import lax
from jax.experimental import pallas as pl
from jax.experimental.pallas import tpu as pltpu

f32, bf16, i32 = jnp.float32, jnp.bfloat16, jnp.int32
HI = lax.Precision.HIGHEST
NT = (((1,), (1,)), ((), ()))
TN = (((0,), (0,)), ((), ()))

D_MODEL = 1024
HEAD_DIM = 64
N_HEADS = 8
ATT_W = N_HEADS * HEAD_DIM
ATT_SCALE = HEAD_DIM ** -0.5
PAGE = 128
MOBA_BLOCK = 256
MOBA_TOPK = 3
T5_BUCKETS = 32
T5_MAX_DIST = 128
SSM_HEADS = 16
SSM_P = 64
SSM_N = 128
SSM_GROUPS = 2
SSM_INNER = 1024
SSM_CONV = 4
CONV_DIM = 1536
N_GROUPS_E = 4
E_PER_GROUP = 8
N_EXPERTS = 32
D_EXPERT = 512
RMS_EPS = 1e-6
VMEM_LIMIT = 56 * 1024 * 1024

C_FQ, C_FK, C_MQ, C_MK, C_FV, C_MV, C_Z, C_XS, C_GL, C_BC, C_END = (
    0, 512, 1024, 1536, 2048, 2560, 3072, 4096, 5120, 8192, 8704)
N_NORMED = 4
N_ATT_TILES = 6
PROJ_TN = 512
S_DT, S_FF, S_W = 0, 16, 128


def _cp(sem):
    return pltpu.CompilerParams(dimension_semantics=sem, vmem_limit_bytes=VMEM_LIMIT)


def _softplus(x):
    return jnp.maximum(x, 0.0) + jnp.log1p(jnp.exp(-jnp.abs(x)))


def _silu(x):
    return x * (1.0 / (1.0 + jnp.exp(-x)))


def _bias_tiles_body(rb_ref, d_ref, p_ref, dt_ref, pt_ref):
    h = pl.program_id(0)
    n = MOBA_BLOCK
    row = lax.broadcasted_iota(i32, (n, n), 0)
    col = lax.broadcasted_iota(i32, (n, n), 1)
    max_exact = T5_BUCKETS // 2
    vals = []
    for dist, off in ((row - col, 0), (row - col, n), (col - row, 0), (col - row, n)):
        d = jnp.maximum(dist + off, 0)
        ratio = jnp.maximum(d, 1).astype(f32) / max_exact
        large = max_exact + (jnp.log(ratio) / math.log(T5_MAX_DIST / max_exact) * (T5_BUCKETS - max_exact)).astype(i32)
        bucket = jnp.where(d < max_exact, d, jnp.minimum(large, T5_BUCKETS - 1))
        val = jnp.zeros((n, n), f32)
        for k in range(T5_BUCKETS):
            val = jnp.where(bucket == k, rb_ref[k, h], val)
        vals.append(val)
    far = vals[1][n - 1:n, 0:1]
    for ref, val in zip((d_ref, p_ref, dt_ref, pt_ref), vals):
        ref[0] = val - far


def bias_tiles(rel_bias):
    n = MOBA_BLOCK
    return pl.pallas_call(
        _bias_tiles_body,
        grid=(N_HEADS,),
        in_specs=[pl.BlockSpec(memory_space=pltpu.SMEM)],
        out_specs=[pl.BlockSpec((1, n, n), lambda h: (h, 0, 0))] * 4,
        out_shape=[jax.ShapeDtypeStruct((N_HEADS, n, n), f32)] * 4,
        compiler_params=_cp(("arbitrary",)),
        name="bias_tiles",
    )(rel_bias)


def _in_proj_body(x_ref, g_ref, w_ref, ws_ref, gain_ref, bd_ref, o_ref, os_ref, o16_ref, h_scr):
    j = pl.program_id(1)

    @pl.when(j == 0)
    def _():
        x = x_ref[...]
        h = x * lax.rsqrt(jnp.mean(x * x, axis=-1, keepdims=True) + RMS_EPS) * g_ref[...]
        hb = h.astype(bf16)
        h_scr[...] = hb
        os_ref[...] = jnp.dot(hb, ws_ref[...], preferred_element_type=f32)

    acc = jnp.dot(h_scr[...], w_ref[...], preferred_element_type=f32)

    @pl.when(j < N_NORMED)
    def _():
        sq = acc * acc
        hi = sq.astype(bf16)
        lo = (sq - hi.astype(f32)).astype(bf16)
        ms = (jnp.dot(hi, bd_ref[...], preferred_element_type=f32)
              + jnp.dot(lo, bd_ref[...], preferred_element_type=f32))
        normed = acc * lax.rsqrt(ms + RMS_EPS) * gain_ref[0]
        o_ref[...] = normed
        o16_ref[...] = normed.astype(bf16)

    @pl.when(j >= N_NORMED)
    def _():
        o_ref[...] = acc

    @pl.when((j >= N_NORMED) & (j < N_ATT_TILES))
    def _():
        o16_ref[...] = acc.astype(bf16)


def in_proj(x, g_mix, w_main, w_small, gains, tm):
    t = x.shape[0]
    nj = C_END // PROJ_TN
    head_avg = jnp.kron(jnp.eye(N_HEADS, dtype=f32), jnp.full((HEAD_DIM, HEAD_DIM), 1.0 / HEAD_DIM, f32)).astype(bf16)
    return pl.pallas_call(
        _in_proj_body,
        grid=(t // tm, nj),
        in_specs=[
            pl.BlockSpec((tm, D_MODEL), lambda i, j: (i, 0)),
            pl.BlockSpec((1, D_MODEL), lambda i, j: (0, 0)),
            pl.BlockSpec((D_MODEL, PROJ_TN), lambda i, j: (0, j)),
            pl.BlockSpec((D_MODEL, S_W), lambda i, j: (0, 0)),
            pl.BlockSpec((1, 1, PROJ_TN), lambda i, j: (jnp.minimum(j, N_NORMED - 1), 0, 0)),
            pl.BlockSpec((PROJ_TN, PROJ_TN), lambda i, j: (0, 0)),
        ],
        out_specs=[
            pl.BlockSpec((tm, PROJ_TN), lambda i, j: (i, j)),
            pl.BlockSpec((tm, S_W), lambda i, j: (i, 0)),
            pl.BlockSpec((tm, PROJ_TN), lambda i, j: (i, jnp.minimum(j, N_ATT_TILES - 1))),
        ],
        out_shape=[jax.ShapeDtypeStruct((t, C_END), f32), jax.ShapeDtypeStruct((t, S_W), f32),
                   jax.ShapeDtypeStruct((t, N_ATT_TILES * PROJ_TN), bf16)],
        scratch_shapes=[pltpu.VMEM((tm, D_MODEL), bf16)],
        compiler_params=_cp(("arbitrary", "arbitrary")),
        name="in_proj",
    )(x, g_mix.reshape(1, D_MODEL), w_main, w_small, gains, head_avg)


def _logf_body(s_ref, b_ref, lf_ref, cum_ref, carry, *, chunks_per_seq):
    i = pl.program_id(0)
    n = s_ref.shape[0]

    @pl.when(i % chunks_per_seq == 0)
    def _():
        carry[...] = jnp.zeros_like(carry)

    ff = s_ref[:, S_FF:S_FF + N_HEADS] + b_ref[...]
    logf = -_softplus(-ff)
    lf_ref[...] = logf
    row = lax.broadcasted_iota(i32, (n, n), 0)
    col = lax.broadcasted_iota(i32, (n, n), 1)
    tri = (row >= col).astype(f32)
    c = jnp.dot(tri, logf, precision=HI, preferred_element_type=f32) + carry[...]
    cum_ref[...] = c
    carry[...] = c[n - 1:n, :]


def logf_cum(small, f_bias, seq_len, tc=256):
    t = small.shape[0]
    return pl.pallas_call(
        functools.partial(_logf_body, chunks_per_seq=seq_len // tc),
        grid=(t // tc,),
        in_specs=[pl.BlockSpec((tc, S_W), lambda i: (i, 0)), pl.BlockSpec((1, N_HEADS), lambda i: (0, 0))],
        out_specs=[pl.BlockSpec((tc, N_HEADS), lambda i: (i, 0))] * 2,
        out_shape=[jax.ShapeDtypeStruct((t, N_HEADS), f32)] * 2,
        scratch_shapes=[pltpu.VMEM((1, N_HEADS), f32)],
        compiler_params=_cp(("arbitrary",)),
        name="logf_cum",
    )(small, f_bias.reshape(1, N_HEADS))


def _half_masks():
    lane = lax.broadcasted_iota(i32, (1, 2 * HEAD_DIM), 1)
    return [lane < HEAD_DIM, lane >= HEAD_DIM]


AUG = 2 * HEAD_DIM
N_TERMS = 3


def _split_terms(x):
    hi = x.astype(bf16)
    r1 = x - hi.astype(f32)
    mid = r1.astype(bf16)
    lo = (r1 - mid.astype(f32)).astype(bf16)
    return [hi, mid, lo]


def _route_terms(x, lane_of):
    hrow = lax.broadcasted_iota(i32, (N_HEADS, AUG), 0)
    lane = lax.broadcasted_iota(i32, (N_HEADS, AUG), 1)
    out = None
    for t, term in enumerate(_split_terms(x)):
        place = (lane == lane_of(t, hrow)).astype(bf16)
        part = jnp.dot(term, place, preferred_element_type=f32)
        out = part if out is None else out + part
    return out


def _values_t(v_ref, vt_scr):
    vt = v_ref[...].astype(f32).T
    sub = lax.broadcasted_iota(i32, (2 * HEAD_DIM, 1), 0)
    vt_scr[0] = jnp.where(sub < HEAD_DIM, vt, 1.0).astype(bf16)
    vt_scr[1] = jnp.where(sub >= HEAD_DIM, vt, 1.0).astype(bf16)


def _kt_step(s_t, m, acc, vt):
    m_new = jnp.maximum(m, jnp.max(s_t, axis=0, keepdims=True))
    alpha = jnp.exp(m - m_new)
    p_t = jnp.exp(s_t - m_new).astype(bf16)
    return m_new, acc * alpha + jnp.dot(vt, p_t, preferred_element_type=f32)


def _kt_init(tq):
    return (jnp.full((1, tq), -jnp.inf, f32), jnp.zeros((2 * HEAD_DIM, tq), f32))


def _kt_output(res):
    a0, a1 = res[0][1], res[1][1]
    sub = lax.broadcasted_iota(i32, (2 * HEAD_DIM, 1), 0)
    out_t = jnp.where(sub < HEAD_DIM, a0 / a0[HEAD_DIM:HEAD_DIM + 1, :], a1 / a1[0:1, :])
    return out_t.T


def _fox_prompt_body(q_ref, k_ref, v_ref, cq_ref, cseq_ref, o_ref, vt_scr, kaug_scr, *, tq):
    pair = pl.program_id(1)
    qi = pl.program_id(2)
    half = _half_masks()
    lane = lax.broadcasted_iota(i32, (1, AUG), 1)
    ones_at = 2 * N_TERMS

    @pl.when(qi == 0)
    def _():
        _values_t(v_ref, vt_scr)
        def lane_of(t, h):
            hh = h - 2 * pair
            return jnp.where((hh == 0) | (hh == 1), N_TERMS * hh + t, -1)

        aug = _route_terms(-cseq_ref[...], lane_of)
        aug = jnp.where((lane >= ones_at) & (lane < ones_at + N_TERMS), 1.0, aug)
        kaug_scr[:, 0:AUG] = k_ref[...]
        kaug_scr[:, AUG:2 * AUG] = aug.astype(bf16)

    q = q_ref[...] * ATT_SCALE
    cq = cq_ref[...]
    qa = []
    for hh in (0, 1):
        aug = _route_terms(cq, lambda t, h, hh=hh: jnp.where(h == 2 * pair + hh, ones_at + t, -1))
        aug = jnp.where((lane >= N_TERMS * hh) & (lane < N_TERMS * (hh + 1)), 1.0, aug)
        qa.append(jnp.concatenate([jnp.where(half[hh], q, 0.0), aug], axis=1).astype(bf16))

    def scores(kb):
        off = pl.multiple_of(kb * tq, tq)
        kt = kaug_scr[pl.ds(off, tq), :]
        return tuple(lax.dot_general(kt, qa[hh], NT, preferred_element_type=f32) for hh in (0, 1))

    def consume(kb, s_pair, carry, causal):
        off = pl.multiple_of(kb * tq, tq)
        out = []
        for hh in (0, 1):
            s_t = s_pair[hh]
            if causal is not None:
                s_t = jnp.where(causal, s_t, -jnp.inf)
            out.append(_kt_step(s_t, *carry[hh], vt_scr[hh, :, pl.ds(off, tq)]))
        return tuple(out)

    def body(kb, c):
        s_pair, carry = c
        s_next = scores(kb + 1)
        return s_next, consume(kb, s_pair, carry, None)

    s_pair, res = lax.fori_loop(0, qi, body, (scores(0), (_kt_init(tq), _kt_init(tq))))
    causal = lax.broadcasted_iota(i32, (tq, tq), 0) <= lax.broadcasted_iota(i32, (tq, tq), 1)
    res = consume(qi, s_pair, res, causal)
    o_ref[...] = _kt_output(res)


def _rank_select(gates):
    out = []
    for n, gn in enumerate(gates):
        rank = jnp.zeros(gn.shape, i32)
        for m, gm in enumerate(gates):
            if m < n:
                rank = rank + (gm >= gn).astype(i32)
            elif m > n:
                rank = rank + (gm > gn).astype(i32)
        out.append(rank < MOBA_TOPK)
    return out


def _moba_prompt_body(q_ref, k_ref, v_ref, bdt_ref, bpt_ref, k32_ref, o_ref, vt_scr, kmean_scr, *, tq, nblk):
    qi = pl.program_id(2)
    half = _half_masks()
    neg = jnp.float32(-jnp.inf)

    @pl.when(qi == 0)
    def _():
        _values_t(v_ref, vt_scr)
        kmean_scr[...] = jnp.mean(k32_ref[...].reshape(nblk, tq, 2 * HEAD_DIM), axis=1)

    q = q_ref[...]
    qs = q * ATT_SCALE
    qh = [jnp.where(half[hh], qs, 0.0).astype(bf16) for hh in (0, 1)]
    kmean = kmean_scr[...]
    idx = lax.broadcasted_iota(i32, (1, nblk), 1)
    sel_t = []
    for hh in (0, 1):
        gate = lax.dot_general(jnp.where(half[hh], q, 0.0), kmean, NT, precision=HI, preferred_element_type=f32)
        g = jnp.where(idx < qi, gate, neg)
        rank = jnp.zeros(g.shape, i32)
        for m in range(nblk):
            gm = g[:, m:m + 1]
            rank = rank + ((gm > g) | ((gm == g) & (m < idx))).astype(i32)
        mask = jnp.where((rank < MOBA_TOPK) & (idx < qi), 0.0, neg)
        sel_t.append(mask.T)

    def mask_row(kb, hh):
        mk = sel_t[hh][0:1, :]
        for n in range(1, nblk):
            mk = jnp.where(kb == n, sel_t[hh][n:n + 1, :], mk)
        return mk

    def scores(kb):
        off = pl.multiple_of(kb * tq, tq)
        kt = k_ref[pl.ds(off, tq), :]
        return tuple(lax.dot_general(kt, qh[hh], NT, preferred_element_type=f32) for hh in (0, 1))

    def consume(kb, s_pair, carry, bias_ref, causal, masked):
        off = pl.multiple_of(kb * tq, tq)
        out = []
        for hh in (0, 1):
            s_t = s_pair[hh]
            if bias_ref is not None:
                s_t = s_t + bias_ref[hh]
            if causal is not None:
                s_t = jnp.where(causal, s_t, neg)
            if masked:
                s_t = s_t + mask_row(kb, hh)
            out.append(_kt_step(s_t, *carry[hh], vt_scr[hh, :, pl.ds(off, tq)]))
        return tuple(out)

    causal = lax.broadcasted_iota(i32, (tq, tq), 0) <= lax.broadcasted_iota(i32, (tq, tq), 1)
    prev = jnp.maximum(qi - 1, 0)
    s_prev = scores(prev)
    res = consume(qi, scores(qi), (_kt_init(tq), _kt_init(tq)), bdt_ref, causal, False)
    s_far = scores(0)
    res = consume(prev, s_prev, res, bpt_ref, None, True)

    def body(kb, c):
        s_pair, carry = c
        s_next = scores(jnp.minimum(kb + 1, nblk - 1))
        return s_next, consume(kb, s_pair, carry, None, None, True)

    _, res = lax.fori_loop(0, qi - 1, body, (s_far, res))
    o_ref[...] = _kt_output(res)


def attn_prompt(mode, proj, qkv16, batch, seq, c_q, c_k, c_v, extra, tq=MOBA_BLOCK):
    nq = seq // tq
    npair = N_HEADS // 2
    pw = 2 * HEAD_DIM
    in_specs = [
        pl.BlockSpec((tq, pw), lambda b, p, i: (b * nq + i, c_q // pw + p)),
        pl.BlockSpec((seq, pw), lambda b, p, i: (b, c_k // pw + p)),
        pl.BlockSpec((seq, pw), lambda b, p, i: (b, c_v // pw + p)),
    ]
    scratch = [pltpu.VMEM((2, pw, seq), bf16)]
    if mode == "fox":
        (cum,) = extra
        body = functools.partial(_fox_prompt_body, tq=tq)
        in_specs += [pl.BlockSpec((tq, N_HEADS), lambda b, p, i: (b * nq + i, 0)),
                     pl.BlockSpec((seq, N_HEADS), lambda b, p, i: (b, 0))]
        args = (cum, cum)
        scratch += [pltpu.VMEM((seq, pw + AUG), bf16)]
    else:
        bias_dt, bias_pt = extra
        body = functools.partial(_moba_prompt_body, tq=tq, nblk=nq)
        in_specs += [pl.BlockSpec((2, tq, tq), lambda b, p, i: (p, 0, 0))] * 2 + [in_specs[1]]
        args = (bias_dt, bias_pt, proj)
        scratch += [pltpu.VMEM((nq, pw), f32)]
    return pl.pallas_call(
        body,
        grid=(batch, npair, nq),
        in_specs=in_specs,
        out_specs=pl.BlockSpec((tq, pw), lambda b, p, i: (b * nq + i, p)),
        out_shape=jax.ShapeDtypeStruct((batch * seq, ATT_W), f32),
        scratch_shapes=scratch,
        compiler_params=_cp(("arbitrary", "arbitrary", "arbitrary")),
        name=f"{mode}_prompt",
    )(proj, qkv16, qkv16, *args)


def _expand_q(q):
    nq = q.shape[0]
    rows = N_HEADS * nq
    qt = jnp.broadcast_to(q[None], (N_HEADS, nq, ATT_W)).reshape(rows, ATT_W)
    rh = lax.broadcasted_iota(i32, (rows, ATT_W), 0) // nq
    lh = lax.broadcasted_iota(i32, (rows, ATT_W), 1) // HEAD_DIM
    return qt, rh == lh


def _rows_from_heads(x_t, nq):
    return jnp.broadcast_to(x_t[:, None, :], (N_HEADS, nq, x_t.shape[1])).reshape(N_HEADS * nq, x_t.shape[1])


def _cum_lanes(lf, carry):
    n = lf.shape[1]
    upper = (lax.broadcasted_iota(i32, (n, n), 0) <= lax.broadcasted_iota(i32, (n, n), 1)).astype(bf16)
    hi = lf.astype(bf16).astype(f32)
    mid = (lf - hi).astype(bf16).astype(f32)
    lo = lf - hi - mid
    parts = jnp.dot(jnp.concatenate([hi, mid, lo], axis=0).astype(bf16), upper, preferred_element_type=f32)
    h = lf.shape[0]
    c = parts[0:h] + parts[h:2 * h] + parts[2 * h:3 * h] + carry
    return c, c[:, n - 1:n]


def _sample_attn_body(pt_ref, q_ref, kn_ref, vn_ref, *refs, mode, npages, nq):
    del pt_ref
    if mode == "fox":
        lfn_ref, refs = refs[0], refs[1:]
    else:
        bo_ref, bl_ref, refs = refs[0], refs[1], refs[2:]
    kp, vp, refs = refs[:npages], refs[npages:2 * npages], refs[2 * npages:]
    if mode == "fox":
        lp, refs = refs[:npages], refs[npages:]
    o_ref = refs[0]
    rows = N_HEADS * nq
    neg = jnp.float32(-jnp.inf)

    q = q_ref[...]
    qt, diag = _expand_q(q)
    qe = jnp.where(diag, qt * ATT_SCALE, 0.0).astype(bf16)
    s_past = [jnp.dot(qe, kp[j][0].astype(bf16), preferred_element_type=f32) for j in range(npages)]
    s_new = lax.dot_general(qe, kn_ref[...].astype(bf16), NT, preferred_element_type=f32)
    qpos = lax.broadcasted_iota(i32, (rows, nq), 0) % nq
    kpos = lax.broadcasted_iota(i32, (rows, nq), 1)

    if mode == "fox":
        carry = jnp.zeros((N_HEADS, 1), f32)
        for j in range(npages):
            c, carry = _cum_lanes(lp[j][0], carry)
            s_past[j] = s_past[j] - _rows_from_heads(c, nq)
        c, _ = _cum_lanes(lfn_ref[...].T, carry)
        s_new = s_new - _rows_from_heads(c, nq)
    else:
        ppb = MOBA_BLOCK // PAGE
        nblk = npages // ppb
        q_t = q.T
        gates = []
        for n in range(nblk):
            ksum = kp[ppb * n][0]
            for j in range(1, ppb):
                ksum = ksum + kp[ppb * n + j][0]
            kmean = jnp.sum(ksum, axis=1, keepdims=True) * (1.0 / MOBA_BLOCK)
            gates.append(jnp.sum((q_t * kmean).reshape(N_HEADS, HEAD_DIM, nq), axis=1))
        picks = _rank_select(gates)
        own_lane = kpos == qpos
        for n in range(nblk):
            spread = _rows_from_heads(jnp.where(picks[n], 0.0, neg), nq)
            mask = jnp.min(jnp.where(own_lane, spread, 0.0), axis=1, keepdims=True)
            for j in range(ppb):
                pg = ppb * n + j
                s = s_past[pg] + mask
                if n == nblk - 1:
                    s = s + bl_ref[:, j * PAGE:(j + 1) * PAGE]
                s_past[pg] = s
        s_new = s_new + bo_ref[...]

    s_new = jnp.where(kpos <= qpos, s_new, neg)
    m = jnp.max(s_new, axis=-1, keepdims=True)
    for j in range(npages):
        m = jnp.maximum(m, jnp.max(s_past[j], axis=-1, keepdims=True))
    p = jnp.exp(s_new - m)
    l = jnp.sum(p, axis=-1, keepdims=True)
    acc = jnp.dot(p.astype(bf16), vn_ref[...].astype(bf16), preferred_element_type=f32)
    for j in range(npages):
        p = jnp.exp(s_past[j] - m)
        l = l + jnp.sum(p, axis=-1, keepdims=True)
        acc = acc + lax.dot_general(p.astype(bf16), vp[j][0].astype(bf16), NT, preferred_element_type=f32)
    out = jnp.where(diag, acc / l, 0.0)
    o_ref[...] = jnp.sum(out.reshape(N_HEADS, nq, ATT_W), axis=0)


def attn_sample(mode, proj, row0, n_seq, nq, c_q, c_k, c_v, pools, page_table, layer, extra):
    k_pool, v_pool = pools
    n_pool = k_pool.shape[1]
    npages = page_table.shape[1]
    assert (npages * PAGE) % MOBA_BLOCK == 0 and nq <= MOBA_BLOCK
    k_t = k_pool.transpose(0, 1, 3, 4, 2).reshape(-1, ATT_W, PAGE)
    v_t = v_pool.transpose(0, 1, 3, 4, 2).reshape(-1, ATT_W, PAGE)
    rb0 = row0 // nq

    def new_spec(c0):
        return pl.BlockSpec((nq, ATT_W), lambda b, pt: (rb0 + b, c0 // ATT_W))

    def page_spec(j, height):
        return pl.BlockSpec((1, height, PAGE), lambda b, pt: (layer * n_pool + pt[b, j], 0, 0))

    rows = N_HEADS * nq
    kv_specs = [page_spec(j, ATT_W) for j in range(npages)] * 2
    if mode == "fox":
        logf_new, lf_pool = extra
        lf_t = lf_pool.transpose(0, 1, 3, 2).reshape(-1, N_HEADS, PAGE)
        in_specs = ([new_spec(c_q), new_spec(c_k), new_spec(c_v), pl.BlockSpec((nq, N_HEADS), lambda b, pt: (rb0 + b, 0))]
                    + kv_specs + [page_spec(j, N_HEADS) for j in range(npages)])
        args = (proj, proj, proj, logf_new) + (k_t,) * npages + (v_t,) * npages + (lf_t,) * npages
    else:
        bias_own, bias_last = extra
        in_specs = ([new_spec(c_q), new_spec(c_k), new_spec(c_v),
                     pl.BlockSpec((rows, nq), lambda b, pt: (0, 0)), pl.BlockSpec((rows, MOBA_BLOCK), lambda b, pt: (0, 0))]
                    + kv_specs)
        args = (proj, proj, proj, bias_own, bias_last) + (k_t,) * npages + (v_t,) * npages
    return pl.pallas_call(
        functools.partial(_sample_attn_body, mode=mode, npages=npages, nq=nq),
        grid_spec=pltpu.PrefetchScalarGridSpec(
            num_scalar_prefetch=1,
            grid=(n_seq,),
            in_specs=in_specs,
            out_specs=pl.BlockSpec((nq, ATT_W), lambda b, pt: (b, 0)),
        ),
        out_shape=jax.ShapeDtypeStruct((n_seq * nq, ATT_W), f32),
        compiler_params=_cp(("arbitrary",)),
        name=f"{mode}_sample",
    )(page_table, *args)


def _ssd_body(z_ref, xs_ref, bc_ref, sm_ref, cs_ref, h0_ref, cw_ref, cb_ref, dtb_ref, alog_ref, dsk_ref, ng_ref,
              y_ref, conv_ref, ssm_ref, xp_scr, ht_scr, yd_scr, *, cl):
    c = pl.program_id(1)
    nc = pl.num_programs(1)
    tail = SSM_CONV - 1
    base = 8
    nh, hp, ns = SSM_HEADS, SSM_P, SSM_N
    gw = SSM_INNER // SSM_GROUPS

    @pl.when(c == 0)
    def _():
        xp_scr[base - tail:base, :] = cs_ref[0]
        ht_scr[...] = h0_ref[0].T

    xp_scr[base:base + cl, 0:SSM_INNER] = xs_ref[...]
    xp_scr[base:base + cl, SSM_INNER:CONV_DIM] = bc_ref[...]
    conv = cb_ref[...]
    for w in range(SSM_CONV):
        conv = conv + xp_scr[pl.ds(base - tail + w, cl), :] * cw_ref[w:w + 1, :]
    new_tail = xp_scr[base + cl - tail:base + cl, :]
    xp_scr[base - tail:base, :] = new_tail
    u = _silu(conv)
    xs = u[:, :SSM_INNER]
    bm = [u[:, SSM_INNER + g * ns:SSM_INNER + (g + 1) * ns] for g in range(SSM_GROUPS)]
    cm = [u[:, SSM_INNER + (SSM_GROUPS + g) * ns:SSM_INNER + (SSM_GROUPS + g + 1) * ns] for g in range(SSM_GROUPS)]

    dt = _softplus(sm_ref[:, S_DT:S_DT + nh] + dtb_ref[...])
    a = dt * (-jnp.exp(alog_ref[...]))
    row = lax.broadcasted_iota(i32, (cl, cl), 0)
    col = lax.broadcasted_iota(i32, (cl, cl), 1)
    causal = row >= col
    acum = jnp.dot(causal.astype(f32), a, precision=HI, preferred_element_type=f32)
    eye = (lax.broadcasted_iota(i32, (nh, nh), 0) == lax.broadcasted_iota(i32, (nh, nh), 1)).astype(f32)
    acum_t = lax.dot_general(eye, acum, NT, precision=HI, preferred_element_type=f32)
    a_end = acum[cl - 1:cl, :]
    spread = (lax.broadcasted_iota(i32, (nh, SSM_INNER), 1) // hp == lax.broadcasted_iota(i32, (nh, SSM_INNER), 0)).astype(f32)

    def lanes(x):
        return jnp.dot(x, spread, precision=HI, preferred_element_type=f32)

    xdt = xs * lanes(dt)
    half = _half_masks()

    cb = [lax.dot_general(cm[g].astype(bf16), bm[g].astype(bf16), NT, preferred_element_type=f32) for g in range(SSM_GROUPS)]
    for pr in range(nh // 2):
        xp_pair = xdt[:, pr * 2 * hp:(pr + 1) * 2 * hp]
        acc = None
        for hh in (0, 1):
            hd = 2 * pr + hh
            seg = acum[:, hd:hd + 1] - acum_t[hd:hd + 1, :]
            decay = jnp.exp(jnp.where(causal, seg, -jnp.inf))
            mm = (cb[hd // (nh // SSM_GROUPS)] * decay).astype(bf16)
            t = jnp.dot(mm, jnp.where(half[hh], xp_pair, 0.0).astype(bf16), preferred_element_type=f32)
            acc = t if acc is None else acc + t
        yd_scr[:, pr * 2 * hp:(pr + 1) * 2 * hp] = acc

    ht = ht_scr[...]
    xw = (xdt * lanes(jnp.exp(a_end - acum))).astype(bf16)
    y_off = []
    st = []
    for g in range(SSM_GROUPS):
        y_off.append(jnp.dot(cm[g].astype(bf16), ht[:, g * gw:(g + 1) * gw].astype(bf16), preferred_element_type=f32))
        st.append(lax.dot_general(bm[g].astype(bf16), xw[:, g * gw:(g + 1) * gw], TN, preferred_element_type=f32))
    from_start = lanes(jnp.exp(acum))
    y = yd_scr[...] + jnp.concatenate(y_off, axis=1) * from_start + xs * dsk_ref[...]
    ht_new = ht * from_start[cl - 1:cl, :] + jnp.concatenate(st, axis=1)
    ht_scr[...] = ht_new

    yg = y * _silu(z_ref[...])
    outs = []
    for g in range(SSM_GROUPS):
        part = yg[:, g * gw:(g + 1) * gw]
        outs.append(part * lax.rsqrt(jnp.mean(part * part, axis=-1, keepdims=True) + RMS_EPS))
    y_ref[...] = jnp.concatenate(outs, axis=1) * ng_ref[...]

    @pl.when(c == nc - 1)
    def _():
        conv_ref[0] = new_tail
        ssm_ref[0] = ht_new.T


def ssd(proj, small, row0, n_seq, seq, cl, conv_state, ssm_state, conv_w, conv_b, dt_bias, a_log, d_skip, norm_g):
    nc = seq // cl
    rb0 = row0 // cl
    h0 = ssm_state.reshape(n_seq, SSM_HEADS * SSM_P, SSM_N)

    def rows(width, c0):
        return pl.BlockSpec((cl, width), lambda b, c: (rb0 + b * nc + c, c0 // width))

    def const(shape):
        return pl.BlockSpec(shape, lambda b, c: (0,) * len(shape))

    y, new_conv, new_ssm = pl.pallas_call(
        functools.partial(_ssd_body, cl=cl),
        grid=(n_seq, nc),
        in_specs=[
            rows(SSM_INNER, C_Z), rows(SSM_INNER, C_XS), rows(CONV_DIM - SSM_INNER, C_BC), rows(S_W, 0),
            pl.BlockSpec((1, SSM_CONV - 1, CONV_DIM), lambda b, c: (b, 0, 0)),
            pl.BlockSpec((1, SSM_HEADS * SSM_P, SSM_N), lambda b, c: (b, 0, 0)),
            const((SSM_CONV, CONV_DIM)), const((1, CONV_DIM)), const((1, SSM_HEADS)), const((1, SSM_HEADS)),
            const((1, SSM_INNER)), const((1, SSM_INNER)),
        ],
        out_specs=[
            pl.BlockSpec((cl, SSM_INNER), lambda b, c: (b * nc + c, 0)),
            pl.BlockSpec((1, SSM_CONV - 1, CONV_DIM), lambda b, c: (b, 0, 0)),
            pl.BlockSpec((1, SSM_HEADS * SSM_P, SSM_N), lambda b, c: (b, 0, 0)),
        ],
        out_shape=[
            jax.ShapeDtypeStruct((n_seq * seq, SSM_INNER), f32),
            jax.ShapeDtypeStruct((n_seq, SSM_CONV - 1, CONV_DIM), f32),
            jax.ShapeDtypeStruct((n_seq, SSM_HEADS * SSM_P, SSM_N), f32),
        ],
        scratch_shapes=[pltpu.VMEM((8 + cl, CONV_DIM), f32), pltpu.VMEM((SSM_N, SSM_INNER), f32),
                        pltpu.VMEM((cl, SSM_INNER), f32)],
        compiler_params=_cp(("arbitrary", "arbitrary")),
        name=f"ssd_cl{cl}",
    )(proj, proj, proj, small, conv_state, h0, conv_w, conv_b.reshape(1, CONV_DIM), dt_bias.reshape(1, SSM_HEADS),
      a_log.reshape(1, SSM_HEADS), jnp.repeat(d_skip, SSM_P).reshape(1, SSM_INNER), norm_g.reshape(1, SSM_INNER))
    return y, new_conv, new_ssm.reshape(n_seq, SSM_HEADS, SSM_P, SSM_N)


def _merge_body(x_ref, ya_ref, ym_ref, yc_ref, ga_ref, gb_ref, gc_ref, wa_ref, wb_ref, wc_ref, wo_ref, o_ref):
    def branch(y_ref, w_ref, g_ref):
        return jax.nn.sigmoid(g_ref[...]) * jnp.dot(y_ref[...].astype(bf16), w_ref[...], preferred_element_type=f32)

    merged = branch(ya_ref, wa_ref, ga_ref) + branch(ym_ref, wb_ref, gb_ref) + branch(yc_ref, wc_ref, gc_ref)
    o_ref[...] = x_ref[...] + jnp.dot(merged.astype(bf16), wo_ref[...], preferred_element_type=f32)


def merge(x, ya, ym, yc, proj, wa, wb, wc, wo, tm):
    t = x.shape[0]
    g0 = C_GL // D_MODEL

    def rows(width, cb=0):
        return pl.BlockSpec((tm, width), lambda i: (i, cb))

    def const(shape):
        return pl.BlockSpec(shape, lambda i: (0, 0))

    return pl.pallas_call(
        _merge_body,
        grid=(t // tm,),
        in_specs=[rows(D_MODEL), rows(ATT_W), rows(ATT_W), rows(SSM_INNER),
                  rows(D_MODEL, g0), rows(D_MODEL, g0 + 1), rows(D_MODEL, g0 + 2),
                  const((ATT_W, D_MODEL)), const((ATT_W, D_MODEL)), const((SSM_INNER, D_MODEL)), const((D_MODEL, D_MODEL))],
        out_specs=rows(D_MODEL),
        out_shape=jax.ShapeDtypeStruct((t, D_MODEL), f32),
        compiler_params=_cp(("arbitrary",)),
        name="merge",
    )(x, ya, ym, yc, proj, proj, proj, wa, wb, wc, wo)


def _router_body(x_ref, g_ref, wr_ref, br_ref, h_ref, eid_ref, rank_ref, ewt_ref, cnt_ref, cnt_scr, *, tm):
    i = pl.program_id(0)

    @pl.when(i == 0)
    def _():
        cnt_scr[...] = jnp.zeros_like(cnt_scr)

    x = x_ref[...]
    h = x * lax.rsqrt(jnp.mean(x * x, axis=-1, keepdims=True) + RMS_EPS) * g_ref[...]
    h_ref[...] = h
    logits = lax.dot_general(wr_ref[...], h, NT, precision=HI, preferred_element_type=f32) + br_ref[...]
    sub = lax.broadcasted_iota(i32, (8, tm), 0)
    gl = jnp.where(sub < N_GROUPS_E, logits[0:8], -jnp.inf)
    gmax = jnp.max(gl, axis=0, keepdims=True)
    g_top = 1.0 / jnp.sum(jnp.exp(gl - gmax), axis=0, keepdims=True)
    g_idx = jnp.min(jnp.where(gl == gmax, sub, 8), axis=0, keepdims=True)
    e_in = jnp.zeros((E_PER_GROUP, tm), f32)
    for g in range(N_GROUPS_E):
        e_in = jnp.where(g_idx == g, logits[8 + g * E_PER_GROUP:8 + (g + 1) * E_PER_GROUP], e_in)
    ex = jnp.exp(e_in - jnp.max(e_in, axis=0, keepdims=True))
    prob = ex / jnp.sum(ex, axis=0, keepdims=True)
    p1 = jnp.max(prob, axis=0, keepdims=True)
    i1 = jnp.min(jnp.where(prob == p1, sub, 8), axis=0, keepdims=True)
    rest = jnp.where(sub == i1, -1.0, prob)
    p2 = jnp.max(rest, axis=0, keepdims=True)
    i2 = jnp.min(jnp.where(rest == p2, sub, 8), axis=0, keepdims=True)
    denom = p1 + p2
    ids = [g_idx * E_PER_GROUP + i1, g_idx * E_PER_GROUP + i2]
    wts = [g_top * p1 / denom, g_top * p2 / denom]
    eid_ref[...] = jnp.concatenate(ids, axis=0)
    ewt_ref[...] = jnp.concatenate(wts + [jnp.zeros((6, tm), f32)], axis=0).T

    esub = lax.broadcasted_iota(i32, (N_EXPERTS, tm), 0)
    oh = [(esub == ids[k]).astype(f32) for k in (0, 1)]
    both = oh[0] + oh[1]
    before = (lax.broadcasted_iota(i32, (tm, tm), 0) < lax.broadcasted_iota(i32, (tm, tm), 1)).astype(bf16)
    pos = jnp.dot(both.astype(bf16), before, preferred_element_type=f32) + cnt_scr[:, 0:1]
    rank_ref[...] = jnp.concatenate([jnp.sum(oh[k] * pos, axis=0, keepdims=True) for k in (0, 1)], axis=0).astype(i32)
    cnt_scr[...] = cnt_scr[...] + jnp.sum(both, axis=1, keepdims=True)
    cnt_ref[...] = cnt_scr[...].astype(i32)


def router(x, g_ffn, w_router_t, b_router, tm):
    t = x.shape[0]
    nr = w_router_t.shape[0]
    return pl.pallas_call(
        functools.partial(_router_body, tm=tm),
        grid=(t // tm,),
        in_specs=[pl.BlockSpec((tm, D_MODEL), lambda i: (i, 0)), pl.BlockSpec((1, D_MODEL), lambda i: (0, 0)),
                  pl.BlockSpec((nr, D_MODEL), lambda i: (0, 0)), pl.BlockSpec((nr, 1), lambda i: (0, 0))],
        out_specs=[pl.BlockSpec((tm, D_MODEL), lambda i: (i, 0)),
                   pl.BlockSpec((2, tm), lambda i: (0, i)), pl.BlockSpec((2, tm), lambda i: (0, i)),
                   pl.BlockSpec((tm, 8), lambda i: (i, 0)),
                   pl.BlockSpec((N_EXPERTS, 128), lambda i: (0, 0))],
        out_shape=[jax.ShapeDtypeStruct((t, D_MODEL), f32),
                   jax.ShapeDtypeStruct((2, t), i32), jax.ShapeDtypeStruct((2, t), i32),
                   jax.ShapeDtypeStruct((t, 8), f32),
                   jax.ShapeDtypeStruct((N_EXPERTS, 128), i32)],
        scratch_shapes=[pltpu.VMEM((N_EXPERTS, 128), f32)],
        compiler_params=_cp(("arbitrary",)),
        name="router",
    )(x, g_ffn.reshape(1, D_MODEL), w_router_t, b_router)


def _dispatch_body(pstart_ref, eid_ref, rank_ref, h_ref, xs_in_ref, xs_ref, dest_ref, sem, *, tm):
    del xs_in_ref

    def copy(t, d):
        return pltpu.make_async_copy(h_ref.at[pl.ds(t, 1), :], xs_ref.at[pl.ds(d, 1), :], sem)

    def issue(t, _):
        for k in (0, 1):
            d = pstart_ref[eid_ref[k, t]] + rank_ref[k, t]
            dest_ref[k, t] = d
            copy(t, d).start()
        return 0

    lax.fori_loop(0, tm, issue, 0)

    def drain(t, _):
        for k in (0, 1):
            copy(t, dest_ref[k, t]).wait()
        return 0

    lax.fori_loop(0, tm, drain, 0)


def dispatch(h, eid, rank, pstart, cap, tm):
    t = h.shape[0]
    smem_rows = pl.BlockSpec((2, tm), lambda i, ps: (0, i), memory_space=pltpu.SMEM)
    return pl.pallas_call(
        functools.partial(_dispatch_body, tm=tm),
        grid_spec=pltpu.PrefetchScalarGridSpec(
            num_scalar_prefetch=1,
            grid=(t // tm,),
            in_specs=[smem_rows, smem_rows, pl.BlockSpec((tm, D_MODEL), lambda i, ps: (i, 0)),
                      pl.BlockSpec(memory_space=pl.ANY)],
            out_specs=[pl.BlockSpec(memory_space=pl.ANY), smem_rows],
            scratch_shapes=[pltpu.SemaphoreType.DMA(())],
        ),
        out_shape=[jax.ShapeDtypeStruct((cap, D_MODEL), f32), jax.ShapeDtypeStruct((2, t), i32)],
        input_output_aliases={4: 0},
        compiler_params=_cp(("arbitrary",)),
        name="moe_dispatch",
    )(pstart, eid, rank, h, jnp.zeros((cap, D_MODEL), f32))


def _experts_body(be_ref, x_ref, w1_ref, w3_ref, w2_ref, o_ref):
    xb = x_ref[...].astype(bf16)
    a = jnp.dot(xb, w1_ref[0], preferred_element_type=f32)
    b = jnp.dot(xb, w3_ref[0], preferred_element_type=f32)
    o_ref[...] = jnp.dot((_silu(a) * b).astype(bf16), w2_ref[0], preferred_element_type=f32)


def experts(xs, blk_expert, w1, w3, w2, blk):
    cap = xs.shape[0]
    return pl.pallas_call(
        _experts_body,
        grid_spec=pltpu.PrefetchScalarGridSpec(
            num_scalar_prefetch=1,
            grid=(cap // blk,),
            in_specs=[pl.BlockSpec((blk, D_MODEL), lambda i, be: (i, 0)),
                      pl.BlockSpec((1, D_MODEL, D_EXPERT), lambda i, be: (be[i], 0, 0)),
                      pl.BlockSpec((1, D_MODEL, D_EXPERT), lambda i, be: (be[i], 0, 0)),
                      pl.BlockSpec((1, D_EXPERT, D_MODEL), lambda i, be: (be[i], 0, 0))],
            out_specs=pl.BlockSpec((blk, D_MODEL), lambda i, be: (i, 0)),
        ),
        out_shape=jax.ShapeDtypeStruct((cap, D_MODEL), f32),
        compiler_params=_cp(("arbitrary",)),
        name="moe_experts",
    )(blk_expert, xs, w1, w3, w2)


def _combine_body(dest_ref, x_ref, ewt_ref, ys_ref, o_ref, buf, sem, *, tm):
    def copy(t, k):
        return pltpu.make_async_copy(ys_ref.at[pl.ds(dest_ref[k, t], 1), :], buf.at[k, pl.ds(t, 1), :], sem)

    def issue(t, _):
        for k in (0, 1):
            copy(t, k).start()
        return 0

    lax.fori_loop(0, tm, issue, 0)

    def drain(t, _):
        for k in (0, 1):
            copy(t, k).wait()
        return 0

    lax.fori_loop(0, tm, drain, 0)
    w = ewt_ref[...]
    o_ref[...] = x_ref[...] + (buf[0] * w[:, 0:1] + buf[1] * w[:, 1:2])


def combine(x, ewt, dest, ys, tm):
    t = x.shape[0]
    return pl.pallas_call(
        functools.partial(_combine_body, tm=tm),
        grid=(t // tm,),
        in_specs=[pl.BlockSpec((2, tm), lambda i: (0, i), memory_space=pltpu.SMEM),
                  pl.BlockSpec((tm, D_MODEL), lambda i: (i, 0)), pl.BlockSpec((tm, 8), lambda i: (i, 0)),
                  pl.BlockSpec(memory_space=pl.ANY)],
        out_specs=pl.BlockSpec((tm, D_MODEL), lambda i: (i, 0)),
        out_shape=jax.ShapeDtypeStruct((t, D_MODEL), f32),
        scratch_shapes=[pltpu.VMEM((2, tm, D_MODEL), f32), pltpu.SemaphoreType.DMA(())],
        compiler_params=_cp(("arbitrary",)),
        name="moe_combine",
    )(dest, x, ewt, ys)


def moe(x, g_ffn, w_router_t, b_router, w1, w3, w2, tm_router, tm_rows, blk):
    t = x.shape[0]
    h, eid, rank, ewt, counts = router(x, g_ffn, w_router_t, b_router, tm_router)
    counts = counts[:, 0]
    padded = (counts + blk - 1) // blk * blk
    pend = jnp.cumsum(padded)
    pstart = (pend - padded).astype(i32)
    nblocks = -(-2 * t // blk) + N_EXPERTS
    blk_start = jnp.arange(nblocks, dtype=i32) * blk
    blk_expert = jnp.minimum(jnp.sum((pend[None, :] <= blk_start[:, None]).astype(i32), axis=1), N_EXPERTS - 1)
    xs, dest = dispatch(h, eid, rank, pstart, nblocks * blk, tm_rows)
    ys = experts(xs, blk_expert, w1, w3, w2, blk)
    return combine(x, ewt, dest, ys, tm_rows)


def _prep_layer(l, w_in, fox_q_gain, fox_k_gain, moba_q_gain, moba_k_gain, router_group_w, router_group_b,
                router_expert_w, router_expert_b):
    w = w_in[l]
    o = [0]
    for s in (ATT_W, ATT_W, ATT_W, N_HEADS, ATT_W, ATT_W, ATT_W, SSM_INNER, CONV_DIM, SSM_HEADS, 3 * D_MODEL):
        o.append(o[-1] + s)
    fq, fk, fv, ff, mq, mk, mv, z, xbc, dtr, gl = [w[:, o[i]:o[i + 1]] for i in range(11)]
    w_main = jnp.concatenate([fq, fk, mq, mk, fv, mv, z, xbc[:, :SSM_INNER], gl, xbc[:, SSM_INNER:]], axis=1).astype(bf16)
    w_small = jnp.concatenate([dtr, ff, jnp.zeros((D_MODEL, S_W - SSM_HEADS - N_HEADS), f32)], axis=1).astype(bf16)
    gains = jnp.stack([jnp.tile(g[l], N_HEADS) for g in (fox_q_gain, fox_k_gain, moba_q_gain, moba_k_gain)]).reshape(4, 1, ATT_W)
    w_router_t = jnp.concatenate([router_group_w[l].T, jnp.zeros((8 - N_GROUPS_E, D_MODEL), f32), router_expert_w[l].T], axis=0)
    b_router = jnp.concatenate([router_group_b[l], jnp.zeros((8 - N_GROUPS_E,), f32), router_expert_b[l]]).reshape(-1, 1)
    return w_main, w_small, gains, w_router_t, b_router


def kernel(x_prompt, x_sample, cache_fox_k, cache_fox_v, cache_fox_logf, cache_moba_k, cache_moba_v, state_conv, state_ssm, page_table, rel_bias, g_mix, w_in, fox_q_gain, fox_k_gain, fox_f_bias, moba_q_gain, moba_k_gain, conv_w, conv_b, dt_bias, a_log, d_skip, ssm_norm_g, w_out_fox, w_out_moba, w_out_ssm, w_o, g_ffn, router_group_w, router_group_b, router_expert_w, router_expert_b, expert_w1, expert_w3, expert_w2):
    bp, lp, _ = x_prompt.shape
    bs, ls, _ = x_sample.shape
    tp, ts = bp * lp, bs * ls
    depth = w_in.shape[0]
    x = jnp.concatenate([x_prompt.reshape(tp, D_MODEL), x_sample.reshape(ts, D_MODEL)], axis=0)

    bias_d, bias_p, bias_dt, bias_pt = bias_tiles(rel_bias)
    rows = N_HEADS * ls
    bias_own = bias_d[:, :ls, :ls].reshape(rows, ls)
    bias_last = bias_p[:, :ls, :].reshape(rows, MOBA_BLOCK)
    zero_conv = jnp.zeros((bp, SSM_CONV - 1, CONV_DIM), f32)
    zero_ssm = jnp.zeros((bp, SSM_HEADS, SSM_P, SSM_N), f32)

    new_p = [[] for _ in range(7)]
    new_s = [[] for _ in range(7)]
    for l in range(depth):
        w_main, w_small, gains, w_router_t, b_router = _prep_layer(
            l, w_in, fox_q_gain, fox_k_gain, moba_q_gain, moba_k_gain, router_group_w, router_group_b,
            router_expert_w, router_expert_b)
        proj, small, qkv16 = in_proj(x, g_mix[l], w_main, w_small, gains, tm=2176)
        logf, cum = logf_cum(small, fox_f_bias[l], lp)

        ya_p = attn_prompt("fox", proj, qkv16, bp, lp, C_FQ, C_FK, C_FV, (cum,))
        ym_p = attn_prompt("moba", proj, qkv16, bp, lp, C_MQ, C_MK, C_MV, (bias_dt, bias_pt))
        ya_s = attn_sample("fox", proj, tp, bs, ls, C_FQ, C_FK, C_FV, (cache_fox_k, cache_fox_v), page_table, l,
                           (logf, cache_fox_logf))
        ym_s = attn_sample("moba", proj, tp, bs, ls, C_MQ, C_MK, C_MV, (cache_moba_k, cache_moba_v), page_table, l,
                           (bias_own, bias_last))
        ssd_w = (conv_w[l], conv_b[l], dt_bias[l], a_log[l], d_skip[l], ssm_norm_g[l])
        yc_p, conv_p, ssm_p = ssd(proj, small, 0, bp, lp, math.gcd(lp, 128), zero_conv, zero_ssm, *ssd_w)
        yc_s, conv_s, ssm_s = ssd(proj, small, tp, bs, ls, math.gcd(ls, 128), state_conv[l], state_ssm[l], *ssd_w)

        x = merge(x, jnp.concatenate([ya_p, ya_s]), jnp.concatenate([ym_p, ym_s]), jnp.concatenate([yc_p, yc_s]), proj,
                  w_out_fox[l].astype(bf16), w_out_moba[l].astype(bf16), w_out_ssm[l].astype(bf16), w_o[l].astype(bf16),
                  tm=256)
        x = moe(x, g_ffn[l], w_router_t, b_router, expert_w1[l].astype(bf16), expert_w3[l].astype(bf16),
                expert_w2[l].astype(bf16), tm_router=512, tm_rows=256, blk=256)

        def heads(c0, r0, b, s):
            return proj[r0:r0 + b * s, c0:c0 + ATT_W].reshape(b, s, N_HEADS, HEAD_DIM)

        for lst, r0, b, s, conv_n, ssm_n in ((new_p, 0, bp, lp, conv_p, ssm_p), (new_s, tp, bs, ls, conv_s, ssm_s)):
            lst[0].append(heads(C_FK, r0, b, s))
            lst[1].append(heads(C_FV, r0, b, s))
            lst[2].append(logf[r0:r0 + b * s].reshape(b, s, N_HEADS))
            lst[3].append(heads(C_MK, r0, b, s))
            lst[4].append(heads(C_MV, r0, b, s))
            lst[5].append(conv_n)
            lst[6].append(ssm_n)

    yp = x[:tp].reshape(bp, lp, D_MODEL)
    ys = x[tp:].reshape(bs, ls, D_MODEL)
    return (yp, ys) + tuple(jnp.stack(a) for a in new_p) + tuple(jnp.stack(a) for a in new_s)
```

```python
import functools
import math

import jax
import jax.numpy as jnp
from jax import lax
from jax.experimental import pallas as pl
from jax.experimental.pallas import tpu as pltpu

f32, bf16, i32 = jnp.float32, jnp.bfloat16, jnp.int32
HI = lax.Precision.HIGHEST
NT = (((1,), (1,)), ((), ()))
TN = (((0,), (0,)), ((), ()))

D_MODEL = 1024
HEAD_DIM = 64
N_HEADS = 8
ATT_W = N_HEADS * HEAD_DIM
ATT_SCALE = HEAD_DIM ** -0.5
PAGE = 128
MOBA_BLOCK = 256
MOBA_TOPK = 3
T5_BUCKETS = 32
T5_MAX_DIST = 128
SSM_HEADS = 16
SSM_P = 64
SSM_N = 128
SSM_GROUPS = 2
SSM_INNER = 1024
SSM_CONV = 4
CONV_DIM = 1536
N_GROUPS_E = 4
E_PER_GROUP = 8
N_EXPERTS = 32
D_EXPERT = 512
RMS_EPS = 1e-6
VMEM_LIMIT = 56 * 1024 * 1024

C_FQ, C_FK, C_MQ, C_MK, C_FV, C_MV, C_Z, C_XS, C_GL, C_BC, C_END = (
    0, 512, 1024, 1536, 2048, 2560, 3072, 4096, 5120, 8192, 8704)
N_NORMED = 4
N_ATT_TILES = 6
PROJ_TN = 512
S_DT, S_FF, S_W = 0, 16, 128


def _cp(sem):
    return pltpu.CompilerParams(dimension_semantics=sem, vmem_limit_bytes=VMEM_LIMIT)


def _softplus(x):
    return jnp.maximum(x, 0.0) + jnp.log1p(jnp.exp(-jnp.abs(x)))


def _silu(x):
    return x * (1.0 / (1.0 + jnp.exp(-x)))


def _bias_tiles_body(rb_ref, d_ref, p_ref, dt_ref, pt_ref):
    h = pl.program_id(0)
    n = MOBA_BLOCK
    row = lax.broadcasted_iota(i32, (n, n), 0)
    col = lax.broadcasted_iota(i32, (n, n), 1)
    max_exact = T5_BUCKETS // 2
    vals = []
    for dist, off in ((row - col, 0), (row - col, n), (col - row, 0), (col - row, n)):
        d = jnp.maximum(dist + off, 0)
        ratio = jnp.maximum(d, 1).astype(f32) / max_exact
        large = max_exact + (jnp.log(ratio) / math.log(T5_MAX_DIST / max_exact) * (T5_BUCKETS - max_exact)).astype(i32)
        bucket = jnp.where(d < max_exact, d, jnp.minimum(large, T5_BUCKETS - 1))
        val = jnp.zeros((n, n), f32)
        for k in range(T5_BUCKETS):
            val = jnp.where(bucket == k, rb_ref[k, h], val)
        vals.append(val)
    far = vals[1][n - 1:n, 0:1]
    for ref, val in zip((d_ref, p_ref, dt_ref, pt_ref), vals):
        ref[0] = val - far


def bias_tiles(rel_bias):
    n = MOBA_BLOCK
    return pl.pallas_call(
        _bias_tiles_body,
        grid=(N_HEADS,),
        in_specs=[pl.BlockSpec(memory_space=pltpu.SMEM)],
        out_specs=[pl.BlockSpec((1, n, n), lambda h: (h, 0, 0))] * 4,
        out_shape=[jax.ShapeDtypeStruct((N_HEADS, n, n), f32)] * 4,
        compiler_params=_cp(("arbitrary",)),
        name="bias_tiles",
    )(rel_bias)


def _in_proj_body(x_ref, g_ref, w_ref, ws_ref, gain_ref, bd_ref, o_ref, os_ref, o16_ref, h_scr):
    j = pl.program_id(1)

    @pl.when(j == 0)
    def _():
        x = x_ref[...]
        h = x * lax.rsqrt(jnp.mean(x * x, axis=-1, keepdims=True) + RMS_EPS) * g_ref[...]
        hb = h.astype(bf16)
        h_scr[...] = hb
        os_ref[...] = jnp.dot(hb, ws_ref[...], preferred_element_type=f32)

    acc = jnp.dot(h_scr[...], w_ref[...], preferred_element_type=f32)

    @pl.when(j < N_NORMED)
    def _():
        sq = acc * acc
        hi = sq.astype(bf16)
        lo = (sq - hi.astype(f32)).astype(bf16)
        ms = (jnp.dot(hi, bd_ref[...], preferred_element_type=f32)
              + jnp.dot(lo, bd_ref[...], preferred_element_type=f32))
        normed = acc * lax.rsqrt(ms + RMS_EPS) * gain_ref[0]
        o_ref[...] = normed
        o16_ref[...] = normed.astype(bf16)

    @pl.when(j >= N_NORMED)
    def _():
        o_ref[...] = acc

    @pl.when((j >= N_NORMED) & (j < N_ATT_TILES))
    def _():
        o16_ref[...] = acc.astype(bf16)


def in_proj(x, g_mix, w_main, w_small, gains, tm):
    t = x.shape[0]
    nj = C_END // PROJ_TN
    head_avg = jnp.kron(jnp.eye(N_HEADS, dtype=f32), jnp.full((HEAD_DIM, HEAD_DIM), 1.0 / HEAD_DIM, f32)).astype(bf16)
    return pl.pallas_call(
        _in_proj_body,
        grid=(t // tm, nj),
        in_specs=[
            pl.BlockSpec((tm, D_MODEL), lambda i, j: (i, 0)),
            pl.BlockSpec((1, D_MODEL), lambda i, j: (0, 0)),
            pl.BlockSpec((D_MODEL, PROJ_TN), lambda i, j: (0, j)),
            pl.BlockSpec((D_MODEL, S_W), lambda i, j: (0, 0)),
            pl.BlockSpec((1, 1, PROJ_TN), lambda i, j: (jnp.minimum(j, N_NORMED - 1), 0, 0)),
            pl.BlockSpec((PROJ_TN, PROJ_TN), lambda i, j: (0, 0)),
        ],
        out_specs=[
            pl.BlockSpec((tm, PROJ_TN), lambda i, j: (i, j)),
            pl.BlockSpec((tm, S_W), lambda i, j: (i, 0)),
            pl.BlockSpec((tm, PROJ_TN), lambda i, j: (i, jnp.minimum(j, N_ATT_TILES - 1))),
        ],
        out_shape=[jax.ShapeDtypeStruct((t, C_END), f32), jax.ShapeDtypeStruct((t, S_W), f32),
                   jax.ShapeDtypeStruct((t, N_ATT_TILES * PROJ_TN), bf16)],
        scratch_shapes=[pltpu.VMEM((tm, D_MODEL), bf16)],
        compiler_params=_cp(("arbitrary", "arbitrary")),
        name="in_proj",
    )(x, g_mix.reshape(1, D_MODEL), w_main, w_small, gains, head_avg)


def _logf_body(s_ref, b_ref, lf_ref, cum_ref, carry, *, chunks_per_seq):
    i = pl.program_id(0)
    n = s_ref.shape[0]

    @pl.when(i % chunks_per_seq == 0)
    def _():
        carry[...] = jnp.zeros_like(carry)

    ff = s_ref[:, S_FF:S_FF + N_HEADS] + b_ref[...]
    logf = -_softplus(-ff)
    lf_ref[...] = logf
    row = lax.broadcasted_iota(i32, (n, n), 0)
    col = lax.broadcasted_iota(i32, (n, n), 1)
    tri = (row >= col).astype(f32)
    c = jnp.dot(tri, logf, precision=HI, preferred_element_type=f32) + carry[...]
    cum_ref[...] = c
    carry[...] = c[n - 1:n, :]


def logf_cum(small, f_bias, seq_len, tc=256):
    t = small.shape[0]
    return pl.pallas_call(
        functools.partial(_logf_body, chunks_per_seq=seq_len // tc),
        grid=(t // tc,),
        in_specs=[pl.BlockSpec((tc, S_W), lambda i: (i, 0)), pl.BlockSpec((1, N_HEADS), lambda i: (0, 0))],
        out_specs=[pl.BlockSpec((tc, N_HEADS), lambda i: (i, 0))] * 2,
        out_shape=[jax.ShapeDtypeStruct((t, N_HEADS), f32)] * 2,
        scratch_shapes=[pltpu.VMEM((1, N_HEADS), f32)],
        compiler_params=_cp(("arbitrary",)),
        name="logf_cum",
    )(small, f_bias.reshape(1, N_HEADS))


def _half_masks():
    lane = lax.broadcasted_iota(i32, (1, 2 * HEAD_DIM), 1)
    return [lane < HEAD_DIM, lane >= HEAD_DIM]


AUG = 2 * HEAD_DIM
N_TERMS = 3


def _split_terms(x):
    hi = x.astype(bf16)
    r1 = x - hi.astype(f32)
    mid = r1.astype(bf16)
    lo = (r1 - mid.astype(f32)).astype(bf16)
    return [hi, mid, lo]


def _route_terms(x, lane_of):
    hrow = lax.broadcasted_iota(i32, (N_HEADS, AUG), 0)
    lane = lax.broadcasted_iota(i32, (N_HEADS, AUG), 1)
    out = None
    for t, term in enumerate(_split_terms(x)):
        place = (lane == lane_of(t, hrow)).astype(bf16)
        part = jnp.dot(term, place, preferred_element_type=f32)
        out = part if out is None else out + part
    return out


def _values_t(v_ref, vt_scr):
    vt = v_ref[...].astype(f32).T
    sub = lax.broadcasted_iota(i32, (2 * HEAD_DIM, 1), 0)
    vt_scr[0] = jnp.where(sub < HEAD_DIM, vt, 1.0).astype(bf16)
    vt_scr[1] = jnp.where(sub >= HEAD_DIM, vt, 1.0).astype(bf16)


def _kt_step(s_t, m, acc, vt):
    m_new = jnp.maximum(m, jnp.max(s_t, axis=0, keepdims=True))
    alpha = jnp.exp(m - m_new)
    p_t = jnp.exp(s_t - m_new).astype(bf16)
    return m_new, acc * alpha + jnp.dot(vt, p_t, preferred_element_type=f32)


def _kt_init(tq):
    return (jnp.full((1, tq), -jnp.inf, f32), jnp.zeros((2 * HEAD_DIM, tq), f32))


def _kt_output(res):
    a0, a1 = res[0][1], res[1][1]
    sub = lax.broadcasted_iota(i32, (2 * HEAD_DIM, 1), 0)
    out_t = jnp.where(sub < HEAD_DIM, a0 / a0[HEAD_DIM:HEAD_DIM + 1, :], a1 / a1[0:1, :])
    return out_t.T


def _fox_prompt_body(q_ref, k_ref, v_ref, cq_ref, cseq_ref, o_ref, vt_scr, kaug_scr, *, tq):
    pair = pl.program_id(1)
    qi = pl.program_id(2)
    half = _half_masks()
    lane = lax.broadcasted_iota(i32, (1, AUG), 1)
    ones_at = 2 * N_TERMS

    @pl.when(qi == 0)
    def _():
        _values_t(v_ref, vt_scr)
        def lane_of(t, h):
            hh = h - 2 * pair
            return jnp.where((hh == 0) | (hh == 1), N_TERMS * hh + t, -1)

        aug = _route_terms(-cseq_ref[...], lane_of)
        aug = jnp.where((lane >= ones_at) & (lane < ones_at + N_TERMS), 1.0, aug)
        kaug_scr[:, 0:AUG] = k_ref[...]
        kaug_scr[:, AUG:2 * AUG] = aug.astype(bf16)

    q = q_ref[...] * ATT_SCALE
    cq = cq_ref[...]
    qa = []
    for hh in (0, 1):
        aug = _route_terms(cq, lambda t, h, hh=hh: jnp.where(h == 2 * pair + hh, ones_at + t, -1))
        aug = jnp.where((lane >= N_TERMS * hh) & (lane < N_TERMS * (hh + 1)), 1.0, aug)
        qa.append(jnp.concatenate([jnp.where(half[hh], q, 0.0), aug], axis=1).astype(bf16))

    def scores(kb):
        off = pl.multiple_of(kb * tq, tq)
        kt = kaug_scr[pl.ds(off, tq), :]
        return tuple(lax.dot_general(kt, qa[hh], NT, preferred_element_type=f32) for hh in (0, 1))

    def consume(kb, s_pair, carry, causal):
        off = pl.multiple_of(kb * tq, tq)
        out = []
        for hh in (0, 1):
            s_t = s_pair[hh]
            if causal is not None:
                s_t = jnp.where(causal, s_t, -jnp.inf)
            out.append(_kt_step(s_t, *carry[hh], vt_scr[hh, :, pl.ds(off, tq)]))
        return tuple(out)

    def body(kb, c):
        s_pair, carry = c
        s_next = scores(kb + 1)
        return s_next, consume(kb, s_pair, carry, None)

    s_pair, res = lax.fori_loop(0, qi, body, (scores(0), (_kt_init(tq), _kt_init(tq))))
    causal = lax.broadcasted_iota(i32, (tq, tq), 0) <= lax.broadcasted_iota(i32, (tq, tq), 1)
    res = consume(qi, s_pair, res, causal)
    o_ref[...] = _kt_output(res)


def _rank_select(gates):
    out = []
    for n, gn in enumerate(gates):
        rank = jnp.zeros(gn.shape, i32)
        for m, gm in enumerate(gates):
            if m < n:
                rank = rank + (gm >= gn).astype(i32)
            elif m > n:
                rank = rank + (gm > gn).astype(i32)
        out.append(rank < MOBA_TOPK)
    return out


def _moba_prompt_body(q_ref, k_ref, v_ref, bdt_ref, bpt_ref, k32_ref, o_ref, vt_scr, kmean_scr, *, tq, nblk):
    qi = pl.program_id(2)
    half = _half_masks()
    neg = jnp.float32(-jnp.inf)

    @pl.when(qi == 0)
    def _():
        _values_t(v_ref, vt_scr)
        kmean_scr[...] = jnp.mean(k32_ref[...].reshape(nblk, tq, 2 * HEAD_DIM), axis=1)

    q = q_ref[...]
    qs = q * ATT_SCALE
    qh = [jnp.where(half[hh], qs, 0.0).astype(bf16) for hh in (0, 1)]
    kmean = kmean_scr[...]
    idx = lax.broadcasted_iota(i32, (1, nblk), 1)
    sel_t = []
    for hh in (0, 1):
        gate = lax.dot_general(jnp.where(half[hh], q, 0.0).astype(bf16), kmean.astype(bf16), NT, preferred_element_type=f32)
        g = jnp.where(idx < qi, gate, neg)
        rank = jnp.zeros(g.shape, i32)
        for m in range(nblk):
            gm = g[:, m:m + 1]
            rank = rank + ((gm > g) | ((gm == g) & (m < idx))).astype(i32)
        mask = jnp.where((rank < MOBA_TOPK) & (idx < qi), 0.0, neg)
        sel_t.append(mask.T)

    def mask_row(kb, hh):
        mk = sel_t[hh][0:1, :]
        for n in range(1, nblk):
            mk = jnp.where(kb == n, sel_t[hh][n:n + 1, :], mk)
        return mk

    def scores(kb):
        off = pl.multiple_of(kb * tq, tq)
        kt = k_ref[pl.ds(off, tq), :]
        return tuple(lax.dot_general(kt, qh[hh], NT, preferred_element_type=f32) for hh in (0, 1))

    def consume(kb, s_pair, carry, bias_ref, causal, masked):
        off = pl.multiple_of(kb * tq, tq)
        out = []
        for hh in (0, 1):
            s_t = s_pair[hh]
            if bias_ref is not None:
                s_t = s_t + bias_ref[hh]
            if causal is not None:
                s_t = jnp.where(causal, s_t, neg)
            if masked:
                s_t = s_t + mask_row(kb, hh)
            out.append(_kt_step(s_t, *carry[hh], vt_scr[hh, :, pl.ds(off, tq)]))
        return tuple(out)

    causal = lax.broadcasted_iota(i32, (tq, tq), 0) <= lax.broadcasted_iota(i32, (tq, tq), 1)
    prev = jnp.maximum(qi - 1, 0)
    s_prev = scores(prev)
    res = consume(qi, scores(qi), (_kt_init(tq), _kt_init(tq)), bdt_ref, causal, False)
    s_far = scores(0)
    res = consume(prev, s_prev, res, bpt_ref, None, True)

    def body(kb, c):
        s_pair, carry = c
        s_next = scores(jnp.minimum(kb + 1, nblk - 1))
        return s_next, consume(kb, s_pair, carry, None, None, True)

    _, res = lax.fori_loop(0, qi - 1, body, (s_far, res))
    o_ref[...] = _kt_output(res)


def attn_prompt(mode, proj, qkv16, batch, seq, c_q, c_k, c_v, extra, tq=MOBA_BLOCK):
    nq = seq // tq
    npair = N_HEADS // 2
    pw = 2 * HEAD_DIM
    in_specs = [
        pl.BlockSpec((tq, pw), lambda b, p, i: (b * nq + i, c_q // pw + p)),
        pl.BlockSpec((seq, pw), lambda b, p, i: (b, c_k // pw + p)),
        pl.BlockSpec((seq, pw), lambda b, p, i: (b, c_v // pw + p)),
    ]
    scratch = [pltpu.VMEM((2, pw, seq), bf16)]
    if mode == "fox":
        (cum,) = extra
        body = functools.partial(_fox_prompt_body, tq=tq)
        in_specs += [pl.BlockSpec((tq, N_HEADS), lambda b, p, i: (b * nq + i, 0)),
                     pl.BlockSpec((seq, N_HEADS), lambda b, p, i: (b, 0))]
        args = (cum, cum)
        scratch += [pltpu.VMEM((seq, pw + AUG), bf16)]
    else:
        bias_dt, bias_pt = extra
        body = functools.partial(_moba_prompt_body, tq=tq, nblk=nq)
        in_specs += [pl.BlockSpec((2, tq, tq), lambda b, p, i: (p, 0, 0))] * 2 + [in_specs[1]]
        args = (bias_dt, bias_pt, proj)
        scratch += [pltpu.VMEM((nq, pw), f32)]
    return pl.pallas_call(
        body,
        grid=(batch, npair, nq),
        in_specs=in_specs,
        out_specs=pl.BlockSpec((tq, pw), lambda b, p, i: (b * nq + i, p)),
        out_shape=jax.ShapeDtypeStruct((batch * seq, ATT_W), f32),
        scratch_shapes=scratch,
        compiler_params=_cp(("arbitrary", "arbitrary", "arbitrary")),
        name=f"{mode}_prompt",
    )(proj, qkv16, qkv16, *args)


def _expand_q(q):
    nq = q.shape[0]
    rows = N_HEADS * nq
    qt = jnp.broadcast_to(q[None], (N_HEADS, nq, ATT_W)).reshape(rows, ATT_W)
    rh = lax.broadcasted_iota(i32, (rows, ATT_W), 0) // nq
    lh = lax.broadcasted_iota(i32, (rows, ATT_W), 1) // HEAD_DIM
    return qt, rh == lh


def _rows_from_heads(x_t, nq):
    return jnp.broadcast_to(x_t[:, None, :], (N_HEADS, nq, x_t.shape[1])).reshape(N_HEADS * nq, x_t.shape[1])


def _cum_lanes(lf, carry):
    n = lf.shape[1]
    upper = (lax.broadcasted_iota(i32, (n, n), 0) <= lax.broadcasted_iota(i32, (n, n), 1)).astype(bf16)
    hi = lf.astype(bf16).astype(f32)
    mid = (lf - hi).astype(bf16).astype(f32)
    lo = lf - hi - mid
    parts = jnp.dot(jnp.concatenate([hi, mid, lo], axis=0).astype(bf16), upper, preferred_element_type=f32)
    h = lf.shape[0]
    c = parts[0:h] + parts[h:2 * h] + parts[2 * h:3 * h] + carry
    return c, c[:, n - 1:n]


def _sample_attn_body(pt_ref, q_ref, kn_ref, vn_ref, *refs, mode, npages, nq):
    del pt_ref
    if mode == "fox":
        lfn_ref, refs = refs[0], refs[1:]
    else:
        bo_ref, bl_ref, refs = refs[0], refs[1], refs[2:]
    kp, vp, refs = refs[:npages], refs[npages:2 * npages], refs[2 * npages:]
    if mode == "fox":
        lp, refs = refs[:npages], refs[npages:]
    o_ref = refs[0]
    rows = N_HEADS * nq
    neg = jnp.float32(-jnp.inf)

    q = q_ref[...]
    qt, diag = _expand_q(q)
    qe = jnp.where(diag, qt * ATT_SCALE, 0.0).astype(bf16)
    s_past = [jnp.dot(qe, kp[j][0].astype(bf16), preferred_element_type=f32) for j in range(npages)]
    s_new = lax.dot_general(qe, kn_ref[...].astype(bf16), NT, preferred_element_type=f32)
    qpos = lax.broadcasted_iota(i32, (rows, nq), 0) % nq
    kpos = lax.broadcasted_iota(i32, (rows, nq), 1)

    if mode == "fox":
        carry = jnp.zeros((N_HEADS, 1), f32)
        for j in range(npages):
            c, carry = _cum_lanes(lp[j][0], carry)
            s_past[j] = s_past[j] - _rows_from_heads(c, nq)
        c, _ = _cum_lanes(lfn_ref[...].T, carry)
        s_new = s_new - _rows_from_heads(c, nq)
    else:
        ppb = MOBA_BLOCK // PAGE
        nblk = npages // ppb
        q_t = q.astype(bf16).astype(f32).T
        gates = []
        for n in range(nblk):
            ksum = kp[ppb * n][0]
            for j in range(1, ppb):
                ksum = ksum + kp[ppb * n + j][0]
            kmean = jnp.sum(ksum, axis=1, keepdims=True) * (1.0 / MOBA_BLOCK)
            kmean = kmean.astype(bf16).astype(f32)
            gates.append(jnp.sum((q_t * kmean).reshape(N_HEADS, HEAD_DIM, nq), axis=1))
        picks = _rank_select(gates)
        own_lane = kpos == qpos
        for n in range(nblk):
            spread = _rows_from_heads(jnp.where(picks[n], 0.0, neg), nq)
            mask = jnp.min(jnp.where(own_lane, spread, 0.0), axis=1, keepdims=True)
            for j in range(ppb):
                pg = ppb * n + j
                s = s_past[pg] + mask
                if n == nblk - 1:
                    s = s + bl_ref[:, j * PAGE:(j + 1) * PAGE]
                s_past[pg] = s
        s_new = s_new + bo_ref[...]

    s_new = jnp.where(kpos <= qpos, s_new, neg)
    m = jnp.max(s_new, axis=-1, keepdims=True)
    for j in range(npages):
        m = jnp.maximum(m, jnp.max(s_past[j], axis=-1, keepdims=True))
    p = jnp.exp(s_new - m)
    l = jnp.sum(p, axis=-1, keepdims=True)
    acc = jnp.dot(p.astype(bf16), vn_ref[...].astype(bf16), preferred_element_type=f32)
    for j in range(npages):
        p = jnp.exp(s_past[j] - m)
        l = l + jnp.sum(p, axis=-1, keepdims=True)
        acc = acc + lax.dot_general(p.astype(bf16), vp[j][0].astype(bf16), NT, preferred_element_type=f32)
    out = jnp.where(diag, acc / l, 0.0)
    o_ref[...] = jnp.sum(out.reshape(N_HEADS, nq, ATT_W), axis=0)


def attn_sample(mode, proj, row0, n_seq, nq, c_q, c_k, c_v, pools, page_table, layer, extra):
    k_pool, v_pool = pools
    n_pool = k_pool.shape[1]
    npages = page_table.shape[1]
    assert (npages * PAGE) % MOBA_BLOCK == 0 and nq <= MOBA_BLOCK
    k_t = k_pool.transpose(0, 1, 3, 4, 2).reshape(-1, ATT_W, PAGE)
    v_t = v_pool.transpose(0, 1, 3, 4, 2).reshape(-1, ATT_W, PAGE)
    rb0 = row0 // nq

    def new_spec(c0):
        return pl.BlockSpec((nq, ATT_W), lambda b, pt: (rb0 + b, c0 // ATT_W))

    def page_spec(j, height):
        return pl.BlockSpec((1, height, PAGE), lambda b, pt: (layer * n_pool + pt[b, j], 0, 0))

    rows = N_HEADS * nq
    kv_specs = [page_spec(j, ATT_W) for j in range(npages)] * 2
    if mode == "fox":
        logf_new, lf_pool = extra
        lf_t = lf_pool.transpose(0, 1, 3, 2).reshape(-1, N_HEADS, PAGE)
        in_specs = ([new_spec(c_q), new_spec(c_k), new_spec(c_v), pl.BlockSpec((nq, N_HEADS), lambda b, pt: (rb0 + b, 0))]
                    + kv_specs + [page_spec(j, N_HEADS) for j in range(npages)])
        args = (proj, proj, proj, logf_new) + (k_t,) * npages + (v_t,) * npages + (lf_t,) * npages
    else:
        bias_own, bias_last = extra
        in_specs = ([new_spec(c_q), new_spec(c_k), new_spec(c_v),
                     pl.BlockSpec((rows, nq), lambda b, pt: (0, 0)), pl.BlockSpec((rows, MOBA_BLOCK), lambda b, pt: (0, 0))]
                    + kv_specs)
        args = (proj, proj, proj, bias_own, bias_last) + (k_t,) * npages + (v_t,) * npages
    return pl.pallas_call(
        functools.partial(_sample_attn_body, mode=mode, npages=npages, nq=nq),
        grid_spec=pltpu.PrefetchScalarGridSpec(
            num_scalar_prefetch=1,
            grid=(n_seq,),
            in_specs=in_specs,
            out_specs=pl.BlockSpec((nq, ATT_W), lambda b, pt: (b, 0)),
        ),
        out_shape=jax.ShapeDtypeStruct((n_seq * nq, ATT_W), f32),
        compiler_params=_cp(("arbitrary",)),
        name=f"{mode}_sample",
    )(page_table, *args)


def _ssd_body(z_ref, xs_ref, bc_ref, sm_ref, cs_ref, h0_ref, cw_ref, cb_ref, dtb_ref, alog_ref, dsk_ref, ng_ref,
              y_ref, conv_ref, ssm_ref, xp_scr, ht_scr, yd_scr, *, cl):
    c = pl.program_id(1)
    nc = pl.num_programs(1)
    tail = SSM_CONV - 1
    base = 8
    nh, hp, ns = SSM_HEADS, SSM_P, SSM_N
    gw = SSM_INNER // SSM_GROUPS

    @pl.when(c == 0)
    def _():
        xp_scr[base - tail:base, :] = cs_ref[0]
        ht_scr[...] = h0_ref[0].T

    xp_scr[base:base + cl, 0:SSM_INNER] = xs_ref[...]
    xp_scr[base:base + cl, SSM_INNER:CONV_DIM] = bc_ref[...]
    conv = cb_ref[...]
    for w in range(SSM_CONV):
        conv = conv + xp_scr[pl.ds(base - tail + w, cl), :] * cw_ref[w:w + 1, :]
    new_tail = xp_scr[base + cl - tail:base + cl, :]
    xp_scr[base - tail:base, :] = new_tail
    u = _silu(conv)
    xs = u[:, :SSM_INNER]
    bm = [u[:, SSM_INNER + g * ns:SSM_INNER + (g + 1) * ns] for g in range(SSM_GROUPS)]
    cm = [u[:, SSM_INNER + (SSM_GROUPS + g) * ns:SSM_INNER + (SSM_GROUPS + g + 1) * ns] for g in range(SSM_GROUPS)]

    dt = _softplus(sm_ref[:, S_DT:S_DT + nh] + dtb_ref[...])
    a = dt * (-jnp.exp(alog_ref[...]))
    row = lax.broadcasted_iota(i32, (cl, cl), 0)
    col = lax.broadcasted_iota(i32, (cl, cl), 1)
    causal = row >= col
    acum = jnp.dot(causal.astype(f32), a, precision=HI, preferred_element_type=f32)
    eye = (lax.broadcasted_iota(i32, (nh, nh), 0) == lax.broadcasted_iota(i32, (nh, nh), 1)).astype(f32)
    acum_t = lax.dot_general(eye, acum, NT, precision=HI, preferred_element_type=f32)
    a_end = acum[cl - 1:cl, :]
    spread = (lax.broadcasted_iota(i32, (nh, SSM_INNER), 1) // hp == lax.broadcasted_iota(i32, (nh, SSM_INNER), 0)).astype(f32)

    def lanes(x):
        return jnp.dot(x, spread, precision=HI, preferred_element_type=f32)

    xdt = xs * lanes(dt)
    half = _half_masks()

    cb = [lax.dot_general(cm[g].astype(bf16), bm[g].astype(bf16), NT, preferred_element_type=f32) for g in range(SSM_GROUPS)]
    for pr in range(nh // 2):
        xp_pair = xdt[:, pr * 2 * hp:(pr + 1) * 2 * hp]
        acc = None
        for hh in (0, 1):
            hd = 2 * pr + hh
            seg = acum[:, hd:hd + 1] - acum_t[hd:hd + 1, :]
            decay = jnp.exp(jnp.where(causal, seg, -jnp.inf))
            mm = (cb[hd // (nh // SSM_GROUPS)] * decay).astype(bf16)
            t = jnp.dot(mm, jnp.where(half[hh], xp_pair, 0.0).astype(bf16), preferred_element_type=f32)
            acc = t if acc is None else acc + t
        yd_scr[:, pr * 2 * hp:(pr + 1) * 2 * hp] = acc

    ht = ht_scr[...]
    xw = (xdt * lanes(jnp.exp(a_end - acum))).astype(bf16)
    y_off = []
    st = []
    for g in range(SSM_GROUPS):
        y_off.append(jnp.dot(cm[g].astype(bf16), ht[:, g * gw:(g + 1) * gw].astype(bf16), preferred_element_type=f32))
        st.append(lax.dot_general(bm[g].astype(bf16), xw[:, g * gw:(g + 1) * gw], TN, preferred_element_type=f32))
    from_start = lanes(jnp.exp(acum))
    y = yd_scr[...] + jnp.concatenate(y_off, axis=1) * from_start + xs * dsk_ref[...]
    ht_new = ht * from_start[cl - 1:cl, :] + jnp.concatenate(st, axis=1)
    ht_scr[...] = ht_new

    yg = y * _silu(z_ref[...])
    outs = []
    for g in range(SSM_GROUPS):
        part = yg[:, g * gw:(g + 1) * gw]
        outs.append(part * lax.rsqrt(jnp.mean(part * part, axis=-1, keepdims=True) + RMS_EPS))
    y_ref[...] = jnp.concatenate(outs, axis=1) * ng_ref[...]

    @pl.when(c == nc - 1)
    def _():
        conv_ref[0] = new_tail
        ssm_ref[0] = ht_new.T


def ssd(proj, small, row0, n_seq, seq, cl, conv_state, ssm_state, conv_w, conv_b, dt_bias, a_log, d_skip, norm_g):
    nc = seq // cl
    rb0 = row0 // cl
    h0 = ssm_state.reshape(n_seq, SSM_HEADS * SSM_P, SSM_N)

    def rows(width, c0):
        return pl.BlockSpec((cl, width), lambda b, c: (rb0 + b * nc + c, c0 // width))

    def const(shape):
        return pl.BlockSpec(shape, lambda b, c: (0,) * len(shape))

    y, new_conv, new_ssm = pl.pallas_call(
        functools.partial(_ssd_body, cl=cl),
        grid=(n_seq, nc),
        in_specs=[
            rows(SSM_INNER, C_Z), rows(SSM_INNER, C_XS), rows(CONV_DIM - SSM_INNER, C_BC), rows(S_W, 0),
            pl.BlockSpec((1, SSM_CONV - 1, CONV_DIM), lambda b, c: (b, 0, 0)),
            pl.BlockSpec((1, SSM_HEADS * SSM_P, SSM_N), lambda b, c: (b, 0, 0)),
            const((SSM_CONV, CONV_DIM)), const((1, CONV_DIM)), const((1, SSM_HEADS)), const((1, SSM_HEADS)),
            const((1, SSM_INNER)), const((1, SSM_INNER)),
        ],
        out_specs=[
            pl.BlockSpec((cl, SSM_INNER), lambda b, c: (b * nc + c, 0)),
            pl.BlockSpec((1, SSM_CONV - 1, CONV_DIM), lambda b, c: (b, 0, 0)),
            pl.BlockSpec((1, SSM_HEADS * SSM_P, SSM_N), lambda b, c: (b, 0, 0)),
        ],
        out_shape=[
            jax.ShapeDtypeStruct((n_seq * seq, SSM_INNER), f32),
            jax.ShapeDtypeStruct((n_seq, SSM_CONV - 1, CONV_DIM), f32),
            jax.ShapeDtypeStruct((n_seq, SSM_HEADS * SSM_P, SSM_N), f32),
        ],
        scratch_shapes=[pltpu.VMEM((8 + cl, CONV_DIM), f32), pltpu.VMEM((SSM_N, SSM_INNER), f32),
                        pltpu.VMEM((cl, SSM_INNER), f32)],
        compiler_params=_cp(("arbitrary", "arbitrary")),
        name=f"ssd_cl{cl}",
    )(proj, proj, proj, small, conv_state, h0, conv_w, conv_b.reshape(1, CONV_DIM), dt_bias.reshape(1, SSM_HEADS),
      a_log.reshape(1, SSM_HEADS), jnp.repeat(d_skip, SSM_P).reshape(1, SSM_INNER), norm_g.reshape(1, SSM_INNER))
    return y, new_conv, new_ssm.reshape(n_seq, SSM_HEADS, SSM_P, SSM_N)


def _merge_body(x_ref, ya_ref, ym_ref, yc_ref, ga_ref, gb_ref, gc_ref, wa_ref, wb_ref, wc_ref, wo_ref, o_ref):
    def branch(y_ref, w_ref, g_ref):
        return jax.nn.sigmoid(g_ref[...]) * jnp.dot(y_ref[...].astype(bf16), w_ref[...], preferred_element_type=f32)

    merged = branch(ya_ref, wa_ref, ga_ref) + branch(ym_ref, wb_ref, gb_ref) + branch(yc_ref, wc_ref, gc_ref)
    o_ref[...] = x_ref[...] + jnp.dot(merged.astype(bf16), wo_ref[...], preferred_element_type=f32)


def merge(x, ya, ym, yc, proj, wa, wb, wc, wo, tm):
    t = x.shape[0]
    g0 = C_GL // D_MODEL

    def rows(width, cb=0):
        return pl.BlockSpec((tm, width), lambda i: (i, cb))

    def const(shape):
        return pl.BlockSpec(shape, lambda i: (0, 0))

    return pl.pallas_call(
        _merge_body,
        grid=(t // tm,),
        in_specs=[rows(D_MODEL), rows(ATT_W), rows(ATT_W), rows(SSM_INNER),
                  rows(D_MODEL, g0), rows(D_MODEL, g0 + 1), rows(D_MODEL, g0 + 2),
                  const((ATT_W, D_MODEL)), const((ATT_W, D_MODEL)), const((SSM_INNER, D_MODEL)), const((D_MODEL, D_MODEL))],
        out_specs=rows(D_MODEL),
        out_shape=jax.ShapeDtypeStruct((t, D_MODEL), f32),
        compiler_params=_cp(("arbitrary",)),
        name="merge",
    )(x, ya, ym, yc, proj, proj, proj, wa, wb, wc, wo)


def _router_body(x_ref, g_ref, wr_ref, br_ref, h_ref, eid_ref, rank_ref, ewt_ref, cnt_ref, cnt_scr, *, tm):
    i = pl.program_id(0)

    @pl.when(i == 0)
    def _():
        cnt_scr[...] = jnp.zeros_like(cnt_scr)

    x = x_ref[...]
    h = x * lax.rsqrt(jnp.mean(x * x, axis=-1, keepdims=True) + RMS_EPS) * g_ref[...]
    h_ref[...] = h
    logits = lax.dot_general(wr_ref[...].astype(bf16), h.astype(bf16), NT, preferred_element_type=f32) + br_ref[...]
    sub = lax.broadcasted_iota(i32, (8, tm), 0)
    gl = jnp.where(sub < N_GROUPS_E, logits[0:8], -jnp.inf)
    gmax = jnp.max(gl, axis=0, keepdims=True)
    g_top = 1.0 / jnp.sum(jnp.exp(gl - gmax), axis=0, keepdims=True)
    g_idx = jnp.min(jnp.where(gl == gmax, sub, 8), axis=0, keepdims=True)
    e_in = jnp.zeros((E_PER_GROUP, tm), f32)
    for g in range(N_GROUPS_E):
        e_in = jnp.where(g_idx == g, logits[8 + g * E_PER_GROUP:8 + (g + 1) * E_PER_GROUP], e_in)
    ex = jnp.exp(e_in - jnp.max(e_in, axis=0, keepdims=True))
    prob = ex / jnp.sum(ex, axis=0, keepdims=True)
    p1 = jnp.max(prob, axis=0, keepdims=True)
    i1 = jnp.min(jnp.where(prob == p1, sub, 8), axis=0, keepdims=True)
    rest = jnp.where(sub == i1, -1.0, prob)
    p2 = jnp.max(rest, axis=0, keepdims=True)
    i2 = jnp.min(jnp.where(rest == p2, sub, 8), axis=0, keepdims=True)
    denom = p1 + p2
    ids = [g_idx * E_PER_GROUP + i1, g_idx * E_PER_GROUP + i2]
    wts = [g_top * p1 / denom, g_top * p2 / denom]
    eid_ref[...] = jnp.concatenate(ids, axis=0)
    ewt_ref[...] = jnp.concatenate(wts + [jnp.zeros((6, tm), f32)], axis=0).T

    esub = lax.broadcasted_iota(i32, (N_EXPERTS, tm), 0)
    oh = [(esub == ids[k]).astype(f32) for k in (0, 1)]
    both = oh[0] + oh[1]
    before = (lax.broadcasted_iota(i32, (tm, tm), 0) < lax.broadcasted_iota(i32, (tm, tm), 1)).astype(bf16)
    pos = jnp.dot(both.astype(bf16), before, preferred_element_type=f32) + cnt_scr[:, 0:1]
    rank_ref[...] = jnp.concatenate([jnp.sum(oh[k] * pos, axis=0, keepdims=True) for k in (0, 1)], axis=0).astype(i32)
    cnt_scr[...] = cnt_scr[...] + jnp.sum(both, axis=1, keepdims=True)
    cnt_ref[...] = cnt_scr[...].astype(i32)


def router(x, g_ffn, w_router_t, b_router, tm):
    t = x.shape[0]
    nr = w_router_t.shape[0]
    return pl.pallas_call(
        functools.partial(_router_body, tm=tm),
        grid=(t // tm,),
        in_specs=[pl.BlockSpec((tm, D_MODEL), lambda i: (i, 0)), pl.BlockSpec((1, D_MODEL), lambda i: (0, 0)),
                  pl.BlockSpec((nr, D_MODEL), lambda i: (0, 0)), pl.BlockSpec((nr, 1), lambda i: (0, 0))],
        out_specs=[pl.BlockSpec((tm, D_MODEL), lambda i: (i, 0)),
                   pl.BlockSpec((2, tm), lambda i: (0, i)), pl.BlockSpec((2, tm), lambda i: (0, i)),
                   pl.BlockSpec((tm, 8), lambda i: (i, 0)),
                   pl.BlockSpec((N_EXPERTS, 128), lambda i: (0, 0))],
        out_shape=[jax.ShapeDtypeStruct((t, D_MODEL), f32),
                   jax.ShapeDtypeStruct((2, t), i32), jax.ShapeDtypeStruct((2, t), i32),
                   jax.ShapeDtypeStruct((t, 8), f32),
                   jax.ShapeDtypeStruct((N_EXPERTS, 128), i32)],
        scratch_shapes=[pltpu.VMEM((N_EXPERTS, 128), f32)],
        compiler_params=_cp(("arbitrary",)),
        name="router",
    )(x, g_ffn.reshape(1, D_MODEL), w_router_t, b_router)


def _dispatch_body(pstart_ref, eid_ref, rank_ref, h_ref, xs_in_ref, xs_ref, dest_ref, sem, *, tm):
    del xs_in_ref
    i = pl.program_id(0)

    def copy(t, d):
        return pltpu.make_async_copy(h_ref.at[pl.ds(t, 1), :], xs_ref.at[pl.ds(d, 1), :], sem)

    def issue(t, _):
        for k in (0, 1):
            d = pstart_ref[eid_ref[k, t]] + rank_ref[k, t]
            dest_ref[k, t] = d
            copy(i * tm + t, d).start()
        return 0

    lax.fori_loop(0, tm, issue, 0)

    def drain(t, _):
        for k in (0, 1):
            copy(0, 0).wait()
        return 0

    @pl.when(i > 0)
    def _():
        lax.fori_loop(0, tm, drain, 0)

    @pl.when(i == pl.num_programs(0) - 1)
    def _():
        lax.fori_loop(0, tm, drain, 0)


def dispatch(h, eid, rank, pstart, cap, tm):
    t = h.shape[0]
    smem_rows = pl.BlockSpec((2, tm), lambda i, ps: (0, i), memory_space=pltpu.SMEM)
    return pl.pallas_call(
        functools.partial(_dispatch_body, tm=tm),
        grid_spec=pltpu.PrefetchScalarGridSpec(
            num_scalar_prefetch=1,
            grid=(t // tm,),
            in_specs=[smem_rows, smem_rows, pl.BlockSpec(memory_space=pl.ANY), pl.BlockSpec(memory_space=pl.ANY)],
            out_specs=[pl.BlockSpec(memory_space=pl.ANY), smem_rows],
            scratch_shapes=[pltpu.SemaphoreType.DMA(())],
        ),
        out_shape=[jax.ShapeDtypeStruct((cap, D_MODEL), f32), jax.ShapeDtypeStruct((2, t), i32)],
        input_output_aliases={4: 0},
        compiler_params=_cp(("arbitrary",)),
        name="moe_dispatch",
    )(pstart, eid, rank, h, jnp.zeros((cap, D_MODEL), f32))


def _experts_body(be_ref, x_ref, w1_ref, w3_ref, w2_ref, o_ref):
    xb = x_ref[...].astype(bf16)
    a = jnp.dot(xb, w1_ref[0].astype(bf16), preferred_element_type=f32)
    b = jnp.dot(xb, w3_ref[0].astype(bf16), preferred_element_type=f32)
    o_ref[...] = jnp.dot((_silu(a) * b).astype(bf16), w2_ref[0].astype(bf16), preferred_element_type=f32)


def experts(xs, blk_expert, w1, w3, w2, blk):
    cap = xs.shape[0]
    return pl.pallas_call(
        _experts_body,
        grid_spec=pltpu.PrefetchScalarGridSpec(
            num_scalar_prefetch=1,
            grid=(cap // blk,),
            in_specs=[pl.BlockSpec((blk, D_MODEL), lambda i, be: (i, 0)),
                      pl.BlockSpec((1, D_MODEL, D_EXPERT), lambda i, be: (be[i], 0, 0)),
                      pl.BlockSpec((1, D_MODEL, D_EXPERT), lambda i, be: (be[i], 0, 0)),
                      pl.BlockSpec((1, D_EXPERT, D_MODEL), lambda i, be: (be[i], 0, 0))],
            out_specs=pl.BlockSpec((blk, D_MODEL), lambda i, be: (i, 0)),
        ),
        out_shape=jax.ShapeDtypeStruct((cap, D_MODEL), f32),
        compiler_params=_cp(("arbitrary",)),
        name="moe_experts",
    )(blk_expert, xs, w1, w3, w2)


def _combine_body(dest_ref, dnext_ref, x_ref, ewt_ref, ys_ref, o_ref, buf, sem, *, tm):
    i = pl.program_id(0)
    slot = i % 2

    def copy(d_ref, s, t, k):
        return pltpu.make_async_copy(ys_ref.at[pl.ds(d_ref[k, t], 1), :], buf.at[s, k, pl.ds(t, 1), :], sem.at[s])

    def issue(d_ref, s):
        def one(t, _):
            for k in (0, 1):
                copy(d_ref, s, t, k).start()
            return 0

        lax.fori_loop(0, tm, one, 0)

    @pl.when(i == 0)
    def _():
        issue(dest_ref, 0)

    @pl.when(i + 1 < pl.num_programs(0))
    def _():
        issue(dnext_ref, 1 - slot)

    def drain(t, _):
        for k in (0, 1):
            copy(dest_ref, slot, t, k).wait()
        return 0

    lax.fori_loop(0, tm, drain, 0)
    w = ewt_ref[...]
    o_ref[...] = x_ref[...] + (buf[slot, 0] * w[:, 0:1] + buf[slot, 1] * w[:, 1:2])


def combine(x, ewt, dest, ys, tm):
    t = x.shape[0]
    nb = t // tm
    return pl.pallas_call(
        functools.partial(_combine_body, tm=tm),
        grid=(nb,),
        in_specs=[pl.BlockSpec((2, tm), lambda i: (0, i), memory_space=pltpu.SMEM),
                  pl.BlockSpec((2, tm), lambda i: (0, jnp.minimum(i + 1, nb - 1)), memory_space=pltpu.SMEM),
                  pl.BlockSpec((tm, D_MODEL), lambda i: (i, 0)), pl.BlockSpec((tm, 8), lambda i: (i, 0)),
                  pl.BlockSpec(memory_space=pl.ANY)],
        out_specs=pl.BlockSpec((tm, D_MODEL), lambda i: (i, 0)),
        out_shape=jax.ShapeDtypeStruct((t, D_MODEL), f32),
        scratch_shapes=[pltpu.VMEM((2, 2, tm, D_MODEL), f32), pltpu.SemaphoreType.DMA((2,))],
        compiler_params=_cp(("arbitrary",)),
        name="moe_combine",
    )(dest, dest, x, ewt, ys)


def moe(x, g_ffn, w_router_t, b_router, layer, w1, w3, w2, tm_router, tm_rows, blk):
    t = x.shape[0]
    h, eid, rank, ewt, counts = router(x, g_ffn, w_router_t, b_router, tm_router)
    counts = counts[:, 0]
    padded = (counts + blk - 1) // blk * blk
    pend = jnp.cumsum(padded)
    pstart = (pend - padded).astype(i32)
    nblocks = -(-2 * t // blk) + N_EXPERTS
    blk_start = jnp.arange(nblocks, dtype=i32) * blk
    blk_expert = jnp.minimum(jnp.sum((pend[None, :] <= blk_start[:, None]).astype(i32), axis=1), N_EXPERTS - 1)
    xs, dest = dispatch(h, eid, rank, pstart, nblocks * blk, tm_rows)
    flat = lambda w: w.reshape((-1,) + w.shape[2:])
    ys = experts(xs, blk_expert + layer * N_EXPERTS, flat(w1), flat(w3), flat(w2), blk)
    return combine(x, ewt, dest, ys, tm_rows)


def _prep_layer(l, w_in, fox_q_gain, fox_k_gain, moba_q_gain, moba_k_gain, router_group_w, router_group_b,
                router_expert_w, router_expert_b):
    w = w_in[l]
    o = [0]
    for s in (ATT_W, ATT_W, ATT_W, N_HEADS, ATT_W, ATT_W, ATT_W, SSM_INNER, CONV_DIM, SSM_HEADS, 3 * D_MODEL):
        o.append(o[-1] + s)
    fq, fk, fv, ff, mq, mk, mv, z, xbc, dtr, gl = [w[:, o[i]:o[i + 1]] for i in range(11)]
    w_main = jnp.concatenate([fq, fk, mq, mk, fv, mv, z, xbc[:, :SSM_INNER], gl, xbc[:, SSM_INNER:]], axis=1).astype(bf16)
    w_small = jnp.concatenate([dtr, ff, jnp.zeros((D_MODEL, S_W - SSM_HEADS - N_HEADS), f32)], axis=1).astype(bf16)
    gains = jnp.stack([jnp.tile(g[l], N_HEADS) for g in (fox_q_gain, fox_k_gain, moba_q_gain, moba_k_gain)]).reshape(4, 1, ATT_W)
    w_router_t = jnp.concatenate([router_group_w[l].T, jnp.zeros((8 - N_GROUPS_E, D_MODEL), f32), router_expert_w[l].T], axis=0)
    b_router = jnp.concatenate([router_group_b[l], jnp.zeros((8 - N_GROUPS_E,), f32), router_expert_b[l]]).reshape(-1, 1)
    return w_main, w_small, gains, w_router_t, b_router


def kernel(x_prompt, x_sample, cache_fox_k, cache_fox_v, cache_fox_logf, cache_moba_k, cache_moba_v, state_conv, state_ssm, page_table, rel_bias, g_mix, w_in, fox_q_gain, fox_k_gain, fox_f_bias, moba_q_gain, moba_k_gain, conv_w, conv_b, dt_bias, a_log, d_skip, ssm_norm_g, w_out_fox, w_out_moba, w_out_ssm, w_o, g_ffn, router_group_w, router_group_b, router_expert_w, router_expert_b, expert_w1, expert_w3, expert_w2):
    bp, lp, _ = x_prompt.shape
    bs, ls, _ = x_sample.shape
    tp, ts = bp * lp, bs * ls
    depth = w_in.shape[0]
    x = jnp.concatenate([x_prompt.reshape(tp, D_MODEL), x_sample.reshape(ts, D_MODEL)], axis=0)

    bias_d, bias_p, bias_dt, bias_pt = bias_tiles(rel_bias)
    rows = N_HEADS * ls
    bias_own = bias_d[:, :ls, :ls].reshape(rows, ls)
    bias_last = bias_p[:, :ls, :].reshape(rows, MOBA_BLOCK)
    zero_conv = jnp.zeros((bp, SSM_CONV - 1, CONV_DIM), f32)
    zero_ssm = jnp.zeros((bp, SSM_HEADS, SSM_P, SSM_N), f32)

    new_p = [[] for _ in range(7)]
    new_s = [[] for _ in range(7)]
    for l in range(depth):
        w_main, w_small, gains, w_router_t, b_router = _prep_layer(
            l, w_in, fox_q_gain, fox_k_gain, moba_q_gain, moba_k_gain, router_group_w, router_group_b,
            router_expert_w, router_expert_b)
        proj, small, qkv16 = in_proj(x, g_mix[l], w_main, w_small, gains, tm=2176)
        logf, cum = logf_cum(small, fox_f_bias[l], lp)

        ya_p = attn_prompt("fox", proj, qkv16, bp, lp, C_FQ, C_FK, C_FV, (cum,))
        ym_p = attn_prompt("moba", proj, qkv16, bp, lp, C_MQ, C_MK, C_MV, (bias_dt, bias_pt))
        ya_s = attn_sample("fox", proj, tp, bs, ls, C_FQ, C_FK, C_FV, (cache_fox_k, cache_fox_v), page_table, l,
                           (logf, cache_fox_logf))
        ym_s = attn_sample("moba", proj, tp, bs, ls, C_MQ, C_MK, C_MV, (cache_moba_k, cache_moba_v), page_table, l,
                           (bias_own, bias_last))
        ssd_w = (conv_w[l], conv_b[l], dt_bias[l], a_log[l], d_skip[l], ssm_norm_g[l])
        yc_p, conv_p, ssm_p = ssd(proj, small, 0, bp, lp, math.gcd(lp, 128), zero_conv, zero_ssm, *ssd_w)
        yc_s, conv_s, ssm_s = ssd(proj, small, tp, bs, ls, math.gcd(ls, 128), state_conv[l], state_ssm[l], *ssd_w)

        x = merge(x, jnp.concatenate([ya_p, ya_s]), jnp.concatenate([ym_p, ym_s]), jnp.concatenate([yc_p, yc_s]), proj,
                  w_out_fox[l].astype(bf16), w_out_moba[l].astype(bf16), w_out_ssm[l].astype(bf16), w_o[l].astype(bf16),
                  tm=256)
        x = moe(x, g_ffn[l], w_router_t, b_router, l, expert_w1, expert_w3, expert_w2,
                tm_router=512, tm_rows=256, blk=256)

        def heads(c0, r0, b, s):
            return proj[r0:r0 + b * s, c0:c0 + ATT_W].reshape(b, s, N_HEADS, HEAD_DIM)

        for lst, r0, b, s, conv_n, ssm_n in ((new_p, 0, bp, lp, conv_p, ssm_p), (new_s, tp, bs, ls, conv_s, ssm_s)):
            lst[0].append(heads(C_FK, r0, b, s))
            lst[1].append(heads(C_FV, r0, b, s))
            lst[2].append(logf[r0:r0 + b * s].reshape(b, s, N_HEADS))
            lst[3].append(heads(C_MK, r0, b, s))
            lst[4].append(heads(C_MV, r0, b, s))
            lst[5].append(conv_n)
            lst[6].append(ssm_n)

    yp = x[:tp].reshape(bp, lp, D_MODEL)
    ys = x[tp:].reshape(bs, ls, D_MODEL)
    return (yp, ys) + tuple(jnp.stack(a) for a in new_p) + tuple(jnp.stack(a) for a in new_s)
```

```python
import functools
import math

import jax
import jax.numpy as jnp
from jax import lax
from jax.experimental import pallas as pl
from jax.experimental.pallas import tpu as pltpu

f32, bf16, i32 = jnp.float32, jnp.bfloat16, jnp.int32
HI = lax.Precision.HIGHEST
NT = (((1,), (1,)), ((), ()))
TN = (((0,), (0,)), ((), ()))

D_MODEL = 1024
HEAD_DIM = 64
N_HEADS = 8
ATT_W = N_HEADS * HEAD_DIM
ATT_SCALE = HEAD_DIM ** -0.5
PAGE = 128
MOBA_BLOCK = 256
MOBA_TOPK = 3
T5_BUCKETS = 32
T5_MAX_DIST = 128
SSM_HEADS = 16
SSM_P = 64
SSM_N = 128
SSM_GROUPS = 2
SSM_INNER = 1024
SSM_CONV = 4
CONV_DIM = 1536
N_GROUPS_E = 4
E_PER_GROUP = 8
N_EXPERTS = 32
D_EXPERT = 512
RMS_EPS = 1e-6
VMEM_LIMIT = 56 * 1024 * 1024

C_FQ, C_FK, C_MQ, C_MK, C_FV, C_MV, C_Z, C_XS, C_GL, C_BC, C_END = (
    0, 512, 1024, 1536, 2048, 2560, 3072, 4096, 5120, 8192, 8704)
N_NORMED = 4
N_ATT_TILES = 6
N_KV_SLOTS = 4
KV_FK, KV_MK, KV_FV, KV_MV = 0, 1, 2, 3
PROJ_TN = 512
S_DT, S_FF, S_W = 0, 16, 128


def _cp(sem):
    return pltpu.CompilerParams(dimension_semantics=sem, vmem_limit_bytes=VMEM_LIMIT)


def _softplus(x):
    return jnp.maximum(x, 0.0) + jnp.log1p(jnp.exp(-jnp.abs(x)))


def _silu(x):
    return x * (1.0 / (1.0 + jnp.exp(-x)))


def _bias_tiles_body(rb_ref, d_ref, p_ref, dt_ref, pt_ref):
    h = pl.program_id(0)
    n = MOBA_BLOCK
    row = lax.broadcasted_iota(i32, (n, n), 0)
    col = lax.broadcasted_iota(i32, (n, n), 1)
    max_exact = T5_BUCKETS // 2
    vals = []
    for dist, off in ((row - col, 0), (row - col, n), (col - row, 0), (col - row, n)):
        d = jnp.maximum(dist + off, 0)
        ratio = jnp.maximum(d, 1).astype(f32) / max_exact
        large = max_exact + (jnp.log(ratio) / math.log(T5_MAX_DIST / max_exact) * (T5_BUCKETS - max_exact)).astype(i32)
        bucket = jnp.where(d < max_exact, d, jnp.minimum(large, T5_BUCKETS - 1))
        val = jnp.zeros((n, n), f32)
        for k in range(T5_BUCKETS):
            val = jnp.where(bucket == k, rb_ref[k, h], val)
        vals.append(val)
    far = vals[1][n - 1:n, 0:1]
    for ref, val in zip((d_ref, p_ref, dt_ref, pt_ref), vals):
        ref[0] = val - far


def bias_tiles(rel_bias):
    n = MOBA_BLOCK
    return pl.pallas_call(
        _bias_tiles_body,
        grid=(N_HEADS,),
        in_specs=[pl.BlockSpec(memory_space=pltpu.SMEM)],
        out_specs=[pl.BlockSpec((1, n, n), lambda h: (h, 0, 0))] * 4,
        out_shape=[jax.ShapeDtypeStruct((N_HEADS, n, n), f32)] * 4,
        compiler_params=_cp(("arbitrary",)),
        name="bias_tiles",
    )(rel_bias)


def _kv_slot(j):
    return jnp.where(j < 3, 0, jnp.minimum(j - 2, N_KV_SLOTS - 1))


def _in_proj_body(x_ref, g_ref, w_ref, ws_ref, gain_ref, bd_ref, o_ref, os_ref, o16_ref, ot_ref, h_scr):
    j = pl.program_id(1)

    @pl.when(j == 0)
    def _():
        x = x_ref[...]
        h = x * lax.rsqrt(jnp.mean(x * x, axis=-1, keepdims=True) + RMS_EPS) * g_ref[...]
        hb = h.astype(bf16)
        h_scr[...] = hb
        os_ref[...] = jnp.dot(hb, ws_ref[...], preferred_element_type=f32)

    acc = jnp.dot(h_scr[...], w_ref[...], preferred_element_type=f32)

    @pl.when(j < N_NORMED)
    def _():
        sq = acc * acc
        hi = sq.astype(bf16)
        lo = (sq - hi.astype(f32)).astype(bf16)
        ms = (jnp.dot(hi, bd_ref[...], preferred_element_type=f32)
              + jnp.dot(lo, bd_ref[...], preferred_element_type=f32))
        normed = acc * lax.rsqrt(ms + RMS_EPS) * gain_ref[0]
        o_ref[...] = normed
        o16_ref[...] = normed.astype(bf16)

        @pl.when((j == 1) | (j == 3))
        def _():
            ot_ref[0] = normed.T

    @pl.when(j >= N_NORMED)
    def _():
        o_ref[...] = acc

    @pl.when((j >= N_NORMED) & (j < N_ATT_TILES))
    def _():
        o16_ref[...] = acc.astype(bf16)
        ot_ref[0] = acc.T


def in_proj(x, g_mix, w_main, w_small, gains, tm):
    t = x.shape[0]
    nj = C_END // PROJ_TN
    head_avg = jnp.kron(jnp.eye(N_HEADS, dtype=f32), jnp.full((HEAD_DIM, HEAD_DIM), 1.0 / HEAD_DIM, f32)).astype(bf16)
    return pl.pallas_call(
        _in_proj_body,
        grid=(t // tm, nj),
        in_specs=[
            pl.BlockSpec((tm, D_MODEL), lambda i, j: (i, 0)),
            pl.BlockSpec((1, D_MODEL), lambda i, j: (0, 0)),
            pl.BlockSpec((D_MODEL, PROJ_TN), lambda i, j: (0, j)),
            pl.BlockSpec((D_MODEL, S_W), lambda i, j: (0, 0)),
            pl.BlockSpec((1, 1, PROJ_TN), lambda i, j: (jnp.minimum(j, N_NORMED - 1), 0, 0)),
            pl.BlockSpec((PROJ_TN, PROJ_TN), lambda i, j: (0, 0)),
        ],
        out_specs=[
            pl.BlockSpec((tm, PROJ_TN), lambda i, j: (i, j)),
            pl.BlockSpec((tm, S_W), lambda i, j: (i, 0)),
            pl.BlockSpec((tm, PROJ_TN), lambda i, j: (i, jnp.minimum(j, N_ATT_TILES - 1))),
            pl.BlockSpec((1, PROJ_TN, tm), lambda i, j: (_kv_slot(j), 0, i)),
        ],
        out_shape=[jax.ShapeDtypeStruct((t, C_END), f32), jax.ShapeDtypeStruct((t, S_W), f32),
                   jax.ShapeDtypeStruct((t, N_ATT_TILES * PROJ_TN), bf16),
                   jax.ShapeDtypeStruct((N_KV_SLOTS, PROJ_TN, t), f32)],
        scratch_shapes=[pltpu.VMEM((tm, D_MODEL), bf16)],
        compiler_params=_cp(("arbitrary", "arbitrary")),
        name="in_proj",
    )(x, g_mix.reshape(1, D_MODEL), w_main, w_small, gains, head_avg)


def _logf_body(s_ref, b_ref, lf_ref, cum_ref, carry, *, chunks_per_seq):
    i = pl.program_id(0)
    n = s_ref.shape[0]

    @pl.when(i % chunks_per_seq == 0)
    def _():
        carry[...] = jnp.zeros_like(carry)

    ff = s_ref[:, S_FF:S_FF + N_HEADS] + b_ref[...]
    logf = -_softplus(-ff)
    lf_ref[...] = logf
    row = lax.broadcasted_iota(i32, (n, n), 0)
    col = lax.broadcasted_iota(i32, (n, n), 1)
    tri = (row >= col).astype(f32)
    c = jnp.dot(tri, logf, precision=HI, preferred_element_type=f32) + carry[...]
    cum_ref[...] = c
    carry[...] = c[n - 1:n, :]


def logf_cum(small, f_bias, seq_len, tc=256):
    t = small.shape[0]
    return pl.pallas_call(
        functools.partial(_logf_body, chunks_per_seq=seq_len // tc),
        grid=(t // tc,),
        in_specs=[pl.BlockSpec((tc, S_W), lambda i: (i, 0)), pl.BlockSpec((1, N_HEADS), lambda i: (0, 0))],
        out_specs=[pl.BlockSpec((tc, N_HEADS), lambda i: (i, 0))] * 2,
        out_shape=[jax.ShapeDtypeStruct((t, N_HEADS), f32)] * 2,
        scratch_shapes=[pltpu.VMEM((1, N_HEADS), f32)],
        compiler_params=_cp(("arbitrary",)),
        name="logf_cum",
    )(small, f_bias.reshape(1, N_HEADS))


def _half_masks():
    lane = lax.broadcasted_iota(i32, (1, 2 * HEAD_DIM), 1)
    return [lane < HEAD_DIM, lane >= HEAD_DIM]


AUG = 2 * HEAD_DIM
N_TERMS = 3


def _split_terms(x):
    hi = x.astype(bf16)
    r1 = x - hi.astype(f32)
    mid = r1.astype(bf16)
    lo = (r1 - mid.astype(f32)).astype(bf16)
    return [hi, mid, lo]


def _route_terms(x, lane_of):
    hrow = lax.broadcasted_iota(i32, (N_HEADS, AUG), 0)
    lane = lax.broadcasted_iota(i32, (N_HEADS, AUG), 1)
    out = None
    for t, term in enumerate(_split_terms(x)):
        place = (lane == lane_of(t, hrow)).astype(bf16)
        part = jnp.dot(term, place, preferred_element_type=f32)
        out = part if out is None else out + part
    return out


def _values_t(v_ref, vt_scr):
    vt = v_ref[...].astype(f32).T
    sub = lax.broadcasted_iota(i32, (2 * HEAD_DIM, 1), 0)
    vt_scr[0] = jnp.where(sub < HEAD_DIM, vt, 1.0).astype(bf16)
    vt_scr[1] = jnp.where(sub >= HEAD_DIM, vt, 1.0).astype(bf16)


def _kt_step(s_t, m, acc, vt):
    m_new = jnp.maximum(m, jnp.max(s_t, axis=0, keepdims=True))
    alpha = jnp.exp(m - m_new)
    p_t = jnp.exp(s_t - m_new).astype(bf16)
    return m_new, acc * alpha + jnp.dot(vt, p_t, preferred_element_type=f32)


def _kt_init(tq):
    return (jnp.full((1, tq), -jnp.inf, f32), jnp.zeros((2 * HEAD_DIM, tq), f32))


def _kt_output(res):
    a0, a1 = res[0][1], res[1][1]
    sub = lax.broadcasted_iota(i32, (2 * HEAD_DIM, 1), 0)
    out_t = jnp.where(sub < HEAD_DIM, a0 / a0[HEAD_DIM:HEAD_DIM + 1, :], a1 / a1[0:1, :])
    return out_t.T


def _fox_prompt_body(q_ref, k_ref, v_ref, cq_ref, cseq_ref, o_ref, vt_scr, kaug_scr, *, tq):
    pair = pl.program_id(1)
    qi = pl.program_id(2)
    half = _half_masks()
    lane = lax.broadcasted_iota(i32, (1, AUG), 1)
    ones_at = 2 * N_TERMS

    @pl.when(qi == 0)
    def _():
        _values_t(v_ref, vt_scr)
        def lane_of(t, h):
            hh = h - 2 * pair
            return jnp.where((hh == 0) | (hh == 1), N_TERMS * hh + t, -1)

        aug = _route_terms(-cseq_ref[...], lane_of)
        aug = jnp.where((lane >= ones_at) & (lane < ones_at + N_TERMS), 1.0, aug)
        kaug_scr[:, 0:AUG] = k_ref[...]
        kaug_scr[:, AUG:2 * AUG] = aug.astype(bf16)

    q = q_ref[...] * ATT_SCALE
    cq = cq_ref[...]
    qa = []
    for hh in (0, 1):
        aug = _route_terms(cq, lambda t, h, hh=hh: jnp.where(h == 2 * pair + hh, ones_at + t, -1))
        aug = jnp.where((lane >= N_TERMS * hh) & (lane < N_TERMS * (hh + 1)), 1.0, aug)
        qa.append(jnp.concatenate([jnp.where(half[hh], q, 0.0), aug], axis=1).astype(bf16))

    def scores(kb):
        off = pl.multiple_of(kb * tq, tq)
        kt = kaug_scr[pl.ds(off, tq), :]
        return tuple(lax.dot_general(kt, qa[hh], NT, preferred_element_type=f32) for hh in (0, 1))

    def consume(kb, s_pair, carry, causal):
        off = pl.multiple_of(kb * tq, tq)
        out = []
        for hh in (0, 1):
            s_t = s_pair[hh]
            if causal is not None:
                s_t = jnp.where(causal, s_t, -jnp.inf)
            out.append(_kt_step(s_t, *carry[hh], vt_scr[hh, :, pl.ds(off, tq)]))
        return tuple(out)

    def body(kb, c):
        s_pair, carry = c
        s_next = scores(kb + 1)
        return s_next, consume(kb, s_pair, carry, None)

    s_pair, res = lax.fori_loop(0, qi, body, (scores(0), (_kt_init(tq), _kt_init(tq))))
    causal = lax.broadcasted_iota(i32, (tq, tq), 0) <= lax.broadcasted_iota(i32, (tq, tq), 1)
    res = consume(qi, s_pair, res, causal)
    o_ref[...] = _kt_output(res)


def _rank_select(gates):
    out = []
    for n, gn in enumerate(gates):
        rank = jnp.zeros(gn.shape, i32)
        for m, gm in enumerate(gates):
            if m < n:
                rank = rank + (gm >= gn).astype(i32)
            elif m > n:
                rank = rank + (gm > gn).astype(i32)
        out.append(rank < MOBA_TOPK)
    return out


def _moba_prompt_body(q_ref, k_ref, v_ref, bdt_ref, bpt_ref, k32_ref, o_ref, vt_scr, kmean_scr, *, tq, nblk):
    qi = pl.program_id(2)
    half = _half_masks()
    neg = jnp.float32(-jnp.inf)

    @pl.when(qi == 0)
    def _():
        _values_t(v_ref, vt_scr)
        kmean_scr[...] = jnp.mean(k32_ref[...].reshape(nblk, tq, 2 * HEAD_DIM), axis=1)

    q = q_ref[...]
    qs = q * ATT_SCALE
    qh = [jnp.where(half[hh], qs, 0.0).astype(bf16) for hh in (0, 1)]
    kmean = kmean_scr[...]
    idx = lax.broadcasted_iota(i32, (1, nblk), 1)
    sel_t = []
    for hh in (0, 1):
        gate = lax.dot_general(jnp.where(half[hh], q, 0.0).astype(bf16), kmean.astype(bf16), NT, preferred_element_type=f32)
        g = jnp.where(idx < qi, gate, neg)
        rank = jnp.zeros(g.shape, i32)
        for m in range(nblk):
            gm = g[:, m:m + 1]
            rank = rank + ((gm > g) | ((gm == g) & (m < idx))).astype(i32)
        mask = jnp.where((rank < MOBA_TOPK) & (idx < qi), 0.0, neg)
        sel_t.append(mask.T)

    def mask_row(kb, hh):
        mk = sel_t[hh][0:1, :]
        for n in range(1, nblk):
            mk = jnp.where(kb == n, sel_t[hh][n:n + 1, :], mk)
        return mk

    def scores(kb):
        off = pl.multiple_of(kb * tq, tq)
        kt = k_ref[pl.ds(off, tq), :]
        return tuple(lax.dot_general(kt, qh[hh], NT, preferred_element_type=f32) for hh in (0, 1))

    def consume(kb, s_pair, carry, bias_ref, causal, masked):
        off = pl.multiple_of(kb * tq, tq)
        out = []
        for hh in (0, 1):
            s_t = s_pair[hh]
            if bias_ref is not None:
                s_t = s_t + bias_ref[hh]
            if causal is not None:
                s_t = jnp.where(causal, s_t, neg)
            if masked:
                s_t = s_t + mask_row(kb, hh)
            out.append(_kt_step(s_t, *carry[hh], vt_scr[hh, :, pl.ds(off, tq)]))
        return tuple(out)

    causal = lax.broadcasted_iota(i32, (tq, tq), 0) <= lax.broadcasted_iota(i32, (tq, tq), 1)
    prev = jnp.maximum(qi - 1, 0)
    s_prev = scores(prev)
    res = consume(qi, scores(qi), (_kt_init(tq), _kt_init(tq)), bdt_ref, causal, False)
    s_far = scores(0)
    res = consume(prev, s_prev, res, bpt_ref, None, True)

    def body(kb, c):
        s_pair, carry = c
        s_next = scores(jnp.minimum(kb + 1, nblk - 1))
        return s_next, consume(kb, s_pair, carry, None, None, True)

    _, res = lax.fori_loop(0, qi - 1, body, (s_far, res))
    o_ref[...] = _kt_output(res)


def attn_prompt(mode, proj, qkv16, batch, seq, c_q, c_k, c_v, extra, tq=MOBA_BLOCK):
    nq = seq // tq
    npair = N_HEADS // 2
    pw = 2 * HEAD_DIM
    in_specs = [
        pl.BlockSpec((tq, pw), lambda b, p, i: (b * nq + i, c_q // pw + p)),
        pl.BlockSpec((seq, pw), lambda b, p, i: (b, c_k // pw + p)),
        pl.BlockSpec((seq, pw), lambda b, p, i: (b, c_v // pw + p)),
    ]
    scratch = [pltpu.VMEM((2, pw, seq), bf16)]
    if mode == "fox":
        (cum,) = extra
        body = functools.partial(_fox_prompt_body, tq=tq)
        in_specs += [pl.BlockSpec((tq, N_HEADS), lambda b, p, i: (b * nq + i, 0)),
                     pl.BlockSpec((seq, N_HEADS), lambda b, p, i: (b, 0))]
        args = (cum, cum)
        scratch += [pltpu.VMEM((seq, pw + AUG), bf16)]
    else:
        bias_dt, bias_pt = extra
        body = functools.partial(_moba_prompt_body, tq=tq, nblk=nq)
        in_specs += [pl.BlockSpec((2, tq, tq), lambda b, p, i: (p, 0, 0))] * 2 + [in_specs[1]]
        args = (bias_dt, bias_pt, proj)
        scratch += [pltpu.VMEM((nq, pw), f32)]
    return pl.pallas_call(
        body,
        grid=(batch, npair, nq),
        in_specs=in_specs,
        out_specs=pl.BlockSpec((tq, pw), lambda b, p, i: (b * nq + i, p)),
        out_shape=jax.ShapeDtypeStruct((batch * seq, ATT_W), f32),
        scratch_shapes=scratch,
        compiler_params=_cp(("arbitrary", "arbitrary", "arbitrary")),
        name=f"{mode}_prompt",
    )(proj, qkv16, qkv16, *args)


def _expand_q(q):
    nq = q.shape[0]
    rows = N_HEADS * nq
    qt = jnp.broadcast_to(q[None], (N_HEADS, nq, ATT_W)).reshape(rows, ATT_W)
    rh = lax.broadcasted_iota(i32, (rows, ATT_W), 0) // nq
    lh = lax.broadcasted_iota(i32, (rows, ATT_W), 1) // HEAD_DIM
    return qt, rh == lh


def _rows_from_heads(x_t, nq):
    return jnp.broadcast_to(x_t[:, None, :], (N_HEADS, nq, x_t.shape[1])).reshape(N_HEADS * nq, x_t.shape[1])


def _cum_lanes(lf, carry):
    n = lf.shape[1]
    upper = (lax.broadcasted_iota(i32, (n, n), 0) <= lax.broadcasted_iota(i32, (n, n), 1)).astype(bf16)
    hi = lf.astype(bf16).astype(f32)
    mid = (lf - hi).astype(bf16).astype(f32)
    lo = lf - hi - mid
    parts = jnp.dot(jnp.concatenate([hi, mid, lo], axis=0).astype(bf16), upper, preferred_element_type=f32)
    h = lf.shape[0]
    c = parts[0:h] + parts[h:2 * h] + parts[2 * h:3 * h] + carry
    return c, c[:, n - 1:n]


def _sample_attn_body(pt_ref, q_ref, kn_ref, vn_ref, *refs, mode, npages, nq):
    del pt_ref
    if mode == "fox":
        lfn_ref, refs = refs[0], refs[1:]
    else:
        bo_ref, bl_ref, refs = refs[0], refs[1], refs[2:]
    kp, vp, refs = refs[:npages], refs[npages:2 * npages], refs[2 * npages:]
    if mode == "fox":
        lp, refs = refs[:npages], refs[npages:]
    o_ref = refs[0]
    rows = N_HEADS * nq
    neg = jnp.float32(-jnp.inf)

    q = q_ref[...]
    qt, diag = _expand_q(q)
    qe = jnp.where(diag, qt * ATT_SCALE, 0.0).astype(bf16)
    s_past = [jnp.dot(qe, kp[j][0].astype(bf16), preferred_element_type=f32) for j in range(npages)]
    s_new = lax.dot_general(qe, kn_ref[...].astype(bf16), NT, preferred_element_type=f32)
    qpos = lax.broadcasted_iota(i32, (rows, nq), 0) % nq
    kpos = lax.broadcasted_iota(i32, (rows, nq), 1)

    if mode == "fox":
        carry = jnp.zeros((N_HEADS, 1), f32)
        for j in range(npages):
            c, carry = _cum_lanes(lp[j][0], carry)
            s_past[j] = s_past[j] - _rows_from_heads(c, nq)
        c, _ = _cum_lanes(lfn_ref[...].T, carry)
        s_new = s_new - _rows_from_heads(c, nq)
    else:
        ppb = MOBA_BLOCK // PAGE
        nblk = npages // ppb
        q_t = q.astype(bf16).astype(f32).T
        gates = []
        for n in range(nblk):
            ksum = kp[ppb * n][0]
            for j in range(1, ppb):
                ksum = ksum + kp[ppb * n + j][0]
            kmean = jnp.sum(ksum, axis=1, keepdims=True) * (1.0 / MOBA_BLOCK)
            kmean = kmean.astype(bf16).astype(f32)
            gates.append(jnp.sum((q_t * kmean).reshape(N_HEADS, HEAD_DIM, nq), axis=1))
        picks = _rank_select(gates)
        own_lane = kpos == qpos
        for n in range(nblk):
            spread = _rows_from_heads(jnp.where(picks[n], 0.0, neg), nq)
            mask = jnp.min(jnp.where(own_lane, spread, 0.0), axis=1, keepdims=True)
            for j in range(ppb):
                pg = ppb * n + j
                s = s_past[pg] + mask
                if n == nblk - 1:
                    s = s + bl_ref[:, j * PAGE:(j + 1) * PAGE]
                s_past[pg] = s
        s_new = s_new + bo_ref[...]

    s_new = jnp.where(kpos <= qpos, s_new, neg)
    m = jnp.max(s_new, axis=-1, keepdims=True)
    for j in range(npages):
        m = jnp.maximum(m, jnp.max(s_past[j], axis=-1, keepdims=True))
    p_new = jnp.exp(s_new - m)
    l = jnp.sum(p_new, axis=-1, keepdims=True)
    p_past = []
    for j in range(npages):
        p_past.append(jnp.exp(s_past[j] - m))
        l = l + jnp.sum(p_past[j], axis=-1, keepdims=True)
    acc = jnp.dot((p_new / l).astype(bf16), vn_ref[...].astype(bf16), preferred_element_type=f32)
    for j in range(npages):
        acc = acc + lax.dot_general((p_past[j] / l).astype(bf16), vp[j][0].astype(bf16), NT, preferred_element_type=f32)
    out = jnp.where(diag, acc, 0.0)
    o_ref[...] = jnp.sum(out.reshape(N_HEADS, nq, ATT_W), axis=0)


def attn_sample(mode, proj, row0, n_seq, nq, c_q, c_k, c_v, pools, page_table, layer, extra):
    k_pool, v_pool = pools
    n_pool = k_pool.shape[1]
    npages = page_table.shape[1]
    assert (npages * PAGE) % MOBA_BLOCK == 0 and nq <= MOBA_BLOCK
    k_t = k_pool.transpose(0, 1, 3, 4, 2).reshape(-1, ATT_W, PAGE)
    v_t = v_pool.transpose(0, 1, 3, 4, 2).reshape(-1, ATT_W, PAGE)
    rb0 = row0 // nq

    def new_spec(c0):
        return pl.BlockSpec((nq, ATT_W), lambda b, pt: (rb0 + b, c0 // ATT_W))

    def page_spec(j, height):
        return pl.BlockSpec((1, height, PAGE), lambda b, pt: (layer * n_pool + pt[b, j], 0, 0))

    rows = N_HEADS * nq
    kv_specs = [page_spec(j, ATT_W) for j in range(npages)] * 2
    if mode == "fox":
        logf_new, lf_pool = extra
        lf_t = lf_pool.transpose(0, 1, 3, 2).reshape(-1, N_HEADS, PAGE)
        in_specs = ([new_spec(c_q), new_spec(c_k), new_spec(c_v), pl.BlockSpec((nq, N_HEADS), lambda b, pt: (rb0 + b, 0))]
                    + kv_specs + [page_spec(j, N_HEADS) for j in range(npages)])
        args = (proj, proj, proj, logf_new) + (k_t,) * npages + (v_t,) * npages + (lf_t,) * npages
    else:
        bias_own, bias_last = extra
        in_specs = ([new_spec(c_q), new_spec(c_k), new_spec(c_v),
                     pl.BlockSpec((rows, nq), lambda b, pt: (0, 0)), pl.BlockSpec((rows, MOBA_BLOCK), lambda b, pt: (0, 0))]
                    + kv_specs)
        args = (proj, proj, proj, bias_own, bias_last) + (k_t,) * npages + (v_t,) * npages
    return pl.pallas_call(
        functools.partial(_sample_attn_body, mode=mode, npages=npages, nq=nq),
        grid_spec=pltpu.PrefetchScalarGridSpec(
            num_scalar_prefetch=1,
            grid=(n_seq,),
            in_specs=in_specs,
            out_specs=pl.BlockSpec((nq, ATT_W), lambda b, pt: (b, 0)),
        ),
        out_shape=jax.ShapeDtypeStruct((n_seq * nq, ATT_W), f32),
        compiler_params=_cp(("arbitrary",)),
        name=f"{mode}_sample",
    )(page_table, *args)


def _ssd_body(z_ref, xs_ref, bc_ref, sm_ref, cs_ref, h0_ref, cw_ref, cb_ref, dtb_ref, alog_ref, dsk_ref, ng_ref,
              y_ref, conv_ref, ssm_ref, xp_scr, ht_scr, yd_scr, *, cl):
    c = pl.program_id(1)
    nc = pl.num_programs(1)
    tail = SSM_CONV - 1
    base = 8
    nh, hp, ns = SSM_HEADS, SSM_P, SSM_N
    gw = SSM_INNER // SSM_GROUPS

    @pl.when(c == 0)
    def _():
        xp_scr[base - tail:base, :] = cs_ref[0]
        ht_scr[...] = h0_ref[0].T

    xp_scr[base:base + cl, 0:SSM_INNER] = xs_ref[...]
    xp_scr[base:base + cl, SSM_INNER:CONV_DIM] = bc_ref[...]
    conv = cb_ref[...]
    for w in range(SSM_CONV):
        conv = conv + xp_scr[pl.ds(base - tail + w, cl), :] * cw_ref[w:w + 1, :]
    new_tail = xp_scr[base + cl - tail:base + cl, :]
    xp_scr[base - tail:base, :] = new_tail
    u = _silu(conv)
    xs = u[:, :SSM_INNER]
    bm = [u[:, SSM_INNER + g * ns:SSM_INNER + (g + 1) * ns] for g in range(SSM_GROUPS)]
    cm = [u[:, SSM_INNER + (SSM_GROUPS + g) * ns:SSM_INNER + (SSM_GROUPS + g + 1) * ns] for g in range(SSM_GROUPS)]

    dt = _softplus(sm_ref[:, S_DT:S_DT + nh] + dtb_ref[...])
    a = dt * (-jnp.exp(alog_ref[...]))
    row = lax.broadcasted_iota(i32, (cl, cl), 0)
    col = lax.broadcasted_iota(i32, (cl, cl), 1)
    causal = row >= col
    acum = jnp.dot(causal.astype(f32), a, precision=HI, preferred_element_type=f32)
    eye = (lax.broadcasted_iota(i32, (nh, nh), 0) == lax.broadcasted_iota(i32, (nh, nh), 1)).astype(f32)
    acum_t = lax.dot_general(eye, acum, NT, precision=HI, preferred_element_type=f32)
    a_end = acum[cl - 1:cl, :]
    spread = (lax.broadcasted_iota(i32, (nh, SSM_INNER), 1) // hp == lax.broadcasted_iota(i32, (nh, SSM_INNER), 0)).astype(f32)

    def lanes(x):
        return jnp.dot(x, spread, precision=HI, preferred_element_type=f32)

    xdt = xs * lanes(dt)
    half = _half_masks()

    cb = [lax.dot_general(cm[g].astype(bf16), bm[g].astype(bf16), NT, preferred_element_type=f32) for g in range(SSM_GROUPS)]
    for pr in range(nh // 2):
        xp_pair = xdt[:, pr * 2 * hp:(pr + 1) * 2 * hp]
        acc = None
        for hh in (0, 1):
            hd = 2 * pr + hh
            seg = acum[:, hd:hd + 1] - acum_t[hd:hd + 1, :]
            decay = jnp.exp(jnp.where(causal, seg, -jnp.inf))
            mm = (cb[hd // (nh // SSM_GROUPS)] * decay).astype(bf16)
            t = jnp.dot(mm, jnp.where(half[hh], xp_pair, 0.0).astype(bf16), preferred_element_type=f32)
            acc = t if acc is None else acc + t
        yd_scr[:, pr * 2 * hp:(pr + 1) * 2 * hp] = acc

    ht = ht_scr[...]
    xw = (xdt * lanes(jnp.exp(a_end - acum))).astype(bf16)
    y_off = []
    st = []
    for g in range(SSM_GROUPS):
        y_off.append(jnp.dot(cm[g].astype(bf16), ht[:, g * gw:(g + 1) * gw].astype(bf16), preferred_element_type=f32))
        st.append(lax.dot_general(bm[g].astype(bf16), xw[:, g * gw:(g + 1) * gw], TN, preferred_element_type=f32))
    from_start = lanes(jnp.exp(acum))
    y = yd_scr[...] + jnp.concatenate(y_off, axis=1) * from_start + xs * dsk_ref[...]
    ht_new = ht * from_start[cl - 1:cl, :] + jnp.concatenate(st, axis=1)
    ht_scr[...] = ht_new

    yg = y * _silu(z_ref[...])
    outs = []
    for g in range(SSM_GROUPS):
        part = yg[:, g * gw:(g + 1) * gw]
        outs.append(part * lax.rsqrt(jnp.mean(part * part, axis=-1, keepdims=True) + RMS_EPS))
    y_ref[...] = jnp.concatenate(outs, axis=1) * ng_ref[...]

    @pl.when(c == nc - 1)
    def _():
        conv_ref[0] = new_tail
        ssm_ref[0] = ht_new.T


def ssd(proj, small, row0, n_seq, seq, cl, layer, conv_state, ssm_state, conv_w, conv_b, dt_bias, a_log, d_skip, norm_g):
    nc = seq // cl
    rb0 = row0 // cl
    s0 = layer * n_seq
    conv_state = conv_state.reshape(-1, SSM_CONV - 1, CONV_DIM)
    h0 = ssm_state.reshape(-1, SSM_HEADS * SSM_P, SSM_N)

    def rows(width, c0):
        return pl.BlockSpec((cl, width), lambda b, c: (rb0 + b * nc + c, c0 // width))

    def const(shape):
        return pl.BlockSpec(shape, lambda b, c: (0,) * len(shape))

    y, new_conv, new_ssm = pl.pallas_call(
        functools.partial(_ssd_body, cl=cl),
        grid=(n_seq, nc),
        in_specs=[
            rows(SSM_INNER, C_Z), rows(SSM_INNER, C_XS), rows(CONV_DIM - SSM_INNER, C_BC), rows(S_W, 0),
            pl.BlockSpec((1, SSM_CONV - 1, CONV_DIM), lambda b, c: (s0 + b, 0, 0)),
            pl.BlockSpec((1, SSM_HEADS * SSM_P, SSM_N), lambda b, c: (s0 + b, 0, 0)),
            const((SSM_CONV, CONV_DIM)), const((1, CONV_DIM)), const((1, SSM_HEADS)), const((1, SSM_HEADS)),
            const((1, SSM_INNER)), const((1, SSM_INNER)),
        ],
        out_specs=[
            pl.BlockSpec((cl, SSM_INNER), lambda b, c: (b * nc + c, 0)),
            pl.BlockSpec((1, SSM_CONV - 1, CONV_DIM), lambda b, c: (b, 0, 0)),
            pl.BlockSpec((1, SSM_HEADS * SSM_P, SSM_N), lambda b, c: (b, 0, 0)),
        ],
        out_shape=[
            jax.ShapeDtypeStruct((n_seq * seq, SSM_INNER), f32),
            jax.ShapeDtypeStruct((n_seq, SSM_CONV - 1, CONV_DIM), f32),
            jax.ShapeDtypeStruct((n_seq, SSM_HEADS * SSM_P, SSM_N), f32),
        ],
        scratch_shapes=[pltpu.VMEM((8 + cl, CONV_DIM), f32), pltpu.VMEM((SSM_N, SSM_INNER), f32),
                        pltpu.VMEM((cl, SSM_INNER), f32)],
        compiler_params=_cp(("arbitrary", "arbitrary")),
        name=f"ssd_cl{cl}",
    )(proj, proj, proj, small, conv_state, h0, conv_w, conv_b.reshape(1, CONV_DIM), dt_bias.reshape(1, SSM_HEADS),
      a_log.reshape(1, SSM_HEADS), jnp.repeat(d_skip, SSM_P).reshape(1, SSM_INNER), norm_g.reshape(1, SSM_INNER))
    return y, new_conv, new_ssm.reshape(n_seq, SSM_HEADS, SSM_P, SSM_N)


def _merge_body(x_ref, ya_ref, ym_ref, yc_ref, ga_ref, gb_ref, gc_ref, wa_ref, wb_ref, wc_ref, wo_ref, o_ref):
    def branch(y_ref, w_ref, g_ref):
        return jax.nn.sigmoid(g_ref[...]) * jnp.dot(y_ref[...].astype(bf16), w_ref[...], preferred_element_type=f32)

    merged = branch(ya_ref, wa_ref, ga_ref) + branch(ym_ref, wb_ref, gb_ref) + branch(yc_ref, wc_ref, gc_ref)
    o_ref[...] = x_ref[...] + jnp.dot(merged.astype(bf16), wo_ref[...], preferred_element_type=f32)


def merge(x, ya, ym, yc, proj, wa, wb, wc, wo, tm):
    t = x.shape[0]
    g0 = C_GL // D_MODEL

    def rows(width, cb=0):
        return pl.BlockSpec((tm, width), lambda i: (i, cb))

    def const(shape):
        return pl.BlockSpec(shape, lambda i: (0, 0))

    return pl.pallas_call(
        _merge_body,
        grid=(t // tm,),
        in_specs=[rows(D_MODEL), rows(ATT_W), rows(ATT_W), rows(SSM_INNER),
                  rows(D_MODEL, g0), rows(D_MODEL, g0 + 1), rows(D_MODEL, g0 + 2),
                  const((ATT_W, D_MODEL)), const((ATT_W, D_MODEL)), const((SSM_INNER, D_MODEL)), const((D_MODEL, D_MODEL))],
        out_specs=rows(D_MODEL),
        out_shape=jax.ShapeDtypeStruct((t, D_MODEL), f32),
        compiler_params=_cp(("arbitrary",)),
        name="merge",
    )(x, ya, ym, yc, proj, proj, proj, wa, wb, wc, wo)


def _router_body(x_ref, g_ref, wr_ref, br_ref, h_ref, eid_ref, rank_ref, ewt_ref, cnt_ref, cnt_scr, *, tm):
    i = pl.program_id(0)

    @pl.when(i == 0)
    def _():
        cnt_scr[...] = jnp.zeros_like(cnt_scr)

    x = x_ref[...]
    h = x * lax.rsqrt(jnp.mean(x * x, axis=-1, keepdims=True) + RMS_EPS) * g_ref[...]
    h_ref[...] = h
    logits = lax.dot_general(wr_ref[...].astype(bf16), h.astype(bf16), NT, preferred_element_type=f32) + br_ref[...]
    sub = lax.broadcasted_iota(i32, (8, tm), 0)
    gl = jnp.where(sub < N_GROUPS_E, logits[0:8], -jnp.inf)
    gmax = jnp.max(gl, axis=0, keepdims=True)
    g_top = 1.0 / jnp.sum(jnp.exp(gl - gmax), axis=0, keepdims=True)
    g_idx = jnp.min(jnp.where(gl == gmax, sub, 8), axis=0, keepdims=True)
    e_in = jnp.zeros((E_PER_GROUP, tm), f32)
    for g in range(N_GROUPS_E):
        e_in = jnp.where(g_idx == g, logits[8 + g * E_PER_GROUP:8 + (g + 1) * E_PER_GROUP], e_in)
    ex = jnp.exp(e_in - jnp.max(e_in, axis=0, keepdims=True))
    prob = ex / jnp.sum(ex, axis=0, keepdims=True)
    p1 = jnp.max(prob, axis=0, keepdims=True)
    i1 = jnp.min(jnp.where(prob == p1, sub, 8), axis=0, keepdims=True)
    rest = jnp.where(sub == i1, -1.0, prob)
    p2 = jnp.max(rest, axis=0, keepdims=True)
    i2 = jnp.min(jnp.where(rest == p2, sub, 8), axis=0, keepdims=True)
    denom = p1 + p2
    ids = [g_idx * E_PER_GROUP + i1, g_idx * E_PER_GROUP + i2]
    wts = [g_top * p1 / denom, g_top * p2 / denom]
    eid_ref[...] = jnp.concatenate(ids, axis=0)
    ewt_ref[...] = jnp.concatenate(wts + [jnp.zeros((6, tm), f32)], axis=0).T

    esub = lax.broadcasted_iota(i32, (N_EXPERTS, tm), 0)
    oh = [(esub == ids[k]).astype(f32) for k in (0, 1)]
    both = oh[0] + oh[1]
    before = (lax.broadcasted_iota(i32, (tm, tm), 0) < lax.broadcasted_iota(i32, (tm, tm), 1)).astype(bf16)
    pos = jnp.dot(both.astype(bf16), before, preferred_element_type=f32) + cnt_scr[:, 0:1]
    rank_ref[...] = jnp.concatenate([jnp.sum(oh[k] * pos, axis=0, keepdims=True) for k in (0, 1)], axis=0).astype(i32)
    cnt_scr[...] = cnt_scr[...] + jnp.sum(both, axis=1, keepdims=True)
    cnt_ref[...] = cnt_scr[...].astype(i32)


def router(x, g_ffn, w_router_t, b_router, tm):
    t = x.shape[0]
    nr = w_router_t.shape[0]
    return pl.pallas_call(
        functools.partial(_router_body, tm=tm),
        grid=(t // tm,),
        in_specs=[pl.BlockSpec((tm, D_MODEL), lambda i: (i, 0)), pl.BlockSpec((1, D_MODEL), lambda i: (0, 0)),
                  pl.BlockSpec((nr, D_MODEL), lambda i: (0, 0)), pl.BlockSpec((nr, 1), lambda i: (0, 0))],
        out_specs=[pl.BlockSpec((tm, D_MODEL), lambda i: (i, 0)),
                   pl.BlockSpec((2, tm), lambda i: (0, i)), pl.BlockSpec((2, tm), lambda i: (0, i)),
                   pl.BlockSpec((tm, 8), lambda i: (i, 0)),
                   pl.BlockSpec((N_EXPERTS, 128), lambda i: (0, 0))],
        out_shape=[jax.ShapeDtypeStruct((t, D_MODEL), f32),
                   jax.ShapeDtypeStruct((2, t), i32), jax.ShapeDtypeStruct((2, t), i32),
                   jax.ShapeDtypeStruct((t, 8), f32),
                   jax.ShapeDtypeStruct((N_EXPERTS, 128), i32)],
        scratch_shapes=[pltpu.VMEM((N_EXPERTS, 128), f32)],
        compiler_params=_cp(("arbitrary",)),
        name="router",
    )(x, g_ffn.reshape(1, D_MODEL), w_router_t, b_router)


def _dispatch_body(pstart_ref, eid_ref, rank_ref, h_ref, xs_in_ref, xs_ref, dest_ref, sem, *, tm):
    del xs_in_ref

    def copy(t, d):
        return pltpu.make_async_copy(h_ref.at[pl.ds(t, 1), :], xs_ref.at[pl.ds(d, 1), :], sem)

    def issue(t, _):
        for k in (0, 1):
            d = pstart_ref[eid_ref[k, t]] + rank_ref[k, t]
            dest_ref[k, t] = d
            copy(t, d).start()
        return 0

    lax.fori_loop(0, tm, issue, 0)

    def drain(t, _):
        for k in (0, 1):
            copy(t, dest_ref[k, t]).wait()
        return 0

    lax.fori_loop(0, tm, drain, 0)


def dispatch(h, eid, rank, pstart, cap, tm):
    t = h.shape[0]
    smem_rows = pl.BlockSpec((2, tm), lambda i, ps: (0, i), memory_space=pltpu.SMEM)
    return pl.pallas_call(
        functools.partial(_dispatch_body, tm=tm),
        grid_spec=pltpu.PrefetchScalarGridSpec(
            num_scalar_prefetch=1,
            grid=(t // tm,),
            in_specs=[smem_rows, smem_rows, pl.BlockSpec((tm, D_MODEL), lambda i, ps: (i, 0)),
                      pl.BlockSpec(memory_space=pl.ANY)],
            out_specs=[pl.BlockSpec(memory_space=pl.ANY), smem_rows],
            scratch_shapes=[pltpu.SemaphoreType.DMA(())],
        ),
        out_shape=[jax.ShapeDtypeStruct((cap, D_MODEL), f32), jax.ShapeDtypeStruct((2, t), i32)],
        input_output_aliases={4: 0},
        compiler_params=_cp(("arbitrary",)),
        name="moe_dispatch",
    )(pstart, eid, rank, h, jnp.zeros((cap, D_MODEL), f32))


def _experts_body(be_ref, nu_ref, x_ref, w1_ref, w3_ref, w2_ref, o_ref):
    @pl.when(pl.program_id(0) < nu_ref[0])
    def _():
        xb = x_ref[...].astype(bf16)
        a = jnp.dot(xb, w1_ref[0].astype(bf16), preferred_element_type=f32)
        b = jnp.dot(xb, w3_ref[0].astype(bf16), preferred_element_type=f32)
        o_ref[...] = jnp.dot((_silu(a) * b).astype(bf16), w2_ref[0].astype(bf16), preferred_element_type=f32)

    @pl.when(pl.program_id(0) >= nu_ref[0])
    def _():
        o_ref[...] = jnp.zeros_like(o_ref)


def experts(xs, blk_expert, n_used, w1, w3, w2, blk):
    cap = xs.shape[0]

    def row(i, be, nu):
        return jnp.minimum(i, nu[0] - 1)

    return pl.pallas_call(
        _experts_body,
        grid_spec=pltpu.PrefetchScalarGridSpec(
            num_scalar_prefetch=2,
            grid=(cap // blk,),
            in_specs=[pl.BlockSpec((blk, D_MODEL), lambda i, be, nu: (row(i, be, nu), 0)),
                      pl.BlockSpec((1, D_MODEL, D_EXPERT), lambda i, be, nu: (be[row(i, be, nu)], 0, 0)),
                      pl.BlockSpec((1, D_MODEL, D_EXPERT), lambda i, be, nu: (be[row(i, be, nu)], 0, 0)),
                      pl.BlockSpec((1, D_EXPERT, D_MODEL), lambda i, be, nu: (be[row(i, be, nu)], 0, 0))],
            out_specs=pl.BlockSpec((blk, D_MODEL), lambda i, be, nu: (i, 0)),
        ),
        out_shape=jax.ShapeDtypeStruct((cap, D_MODEL), f32),
        compiler_params=_cp(("arbitrary",)),
        name="moe_experts",
    )(blk_expert, n_used, xs, w1, w3, w2)


def _combine_body(dest_ref, x_ref, ewt_ref, ys_ref, o_ref, buf, sem, *, tm):
    def copy(t, k):
        return pltpu.make_async_copy(ys_ref.at[pl.ds(dest_ref[k, t], 1), :], buf.at[k, pl.ds(t, 1), :], sem)

    def issue(t, _):
        for k in (0, 1):
            copy(t, k).start()
        return 0

    lax.fori_loop(0, tm, issue, 0)

    def drain(t, _):
        for k in (0, 1):
            copy(t, k).wait()
        return 0

    lax.fori_loop(0, tm, drain, 0)
    w = ewt_ref[...]
    o_ref[...] = x_ref[...] + (buf[0] * w[:, 0:1] + buf[1] * w[:, 1:2])


def combine(x, ewt, dest, ys, tm):
    t = x.shape[0]
    return pl.pallas_call(
        functools.partial(_combine_body, tm=tm),
        grid=(t // tm,),
        in_specs=[pl.BlockSpec((2, tm), lambda i: (0, i), memory_space=pltpu.SMEM),
                  pl.BlockSpec((tm, D_MODEL), lambda i: (i, 0)), pl.BlockSpec((tm, 8), lambda i: (i, 0)),
                  pl.BlockSpec(memory_space=pl.ANY)],
        out_specs=pl.BlockSpec((tm, D_MODEL), lambda i: (i, 0)),
        out_shape=jax.ShapeDtypeStruct((t, D_MODEL), f32),
        scratch_shapes=[pltpu.VMEM((2, tm, D_MODEL), f32), pltpu.SemaphoreType.DMA(())],
        compiler_params=_cp(("arbitrary",)),
        name="moe_combine",
    )(dest, x, ewt, ys)


def moe(x, g_ffn, w_router_t, b_router, layer, w1, w3, w2, tm_router, tm_rows, blk):
    t = x.shape[0]
    h, eid, rank, ewt, counts = router(x, g_ffn, w_router_t, b_router, tm_router)
    counts = counts[:, 0]
    padded = (counts + blk - 1) // blk * blk
    pend = jnp.cumsum(padded)
    pstart = (pend - padded).astype(i32)
    nblocks = -(-2 * t // blk) + N_EXPERTS
    blk_start = jnp.arange(nblocks, dtype=i32) * blk
    blk_expert = jnp.minimum(jnp.sum((pend[None, :] <= blk_start[:, None]).astype(i32), axis=1), N_EXPERTS - 1)
    xs, dest = dispatch(h, eid, rank, pstart, nblocks * blk, tm_rows)
    flat = lambda w: w.reshape((-1,) + w.shape[2:])
    n_used = (pend[N_EXPERTS - 1:] // blk).astype(i32)
    ys = experts(xs, blk_expert + layer * N_EXPERTS, n_used, flat(w1), flat(w3), flat(w2), blk)
    return combine(x, ewt, dest, ys, tm_rows)


def _prep_layer(l, w_in, fox_q_gain, fox_k_gain, moba_q_gain, moba_k_gain, router_group_w, router_group_b,
                router_expert_w, router_expert_b):
    w = w_in[l]
    o = [0]
    for s in (ATT_W, ATT_W, ATT_W, N_HEADS, ATT_W, ATT_W, ATT_W, SSM_INNER, CONV_DIM, SSM_HEADS, 3 * D_MODEL):
        o.append(o[-1] + s)
    fq, fk, fv, ff, mq, mk, mv, z, xbc, dtr, gl = [w[:, o[i]:o[i + 1]] for i in range(11)]
    w_main = jnp.concatenate([fq, fk, mq, mk, fv, mv, z, xbc[:, :SSM_INNER], gl, xbc[:, SSM_INNER:]], axis=1).astype(bf16)
    w_small = jnp.concatenate([dtr, ff, jnp.zeros((D_MODEL, S_W - SSM_HEADS - N_HEADS), f32)], axis=1).astype(bf16)
    gains = jnp.stack([jnp.tile(g[l], N_HEADS) for g in (fox_q_gain, fox_k_gain, moba_q_gain, moba_k_gain)]).reshape(4, 1, ATT_W)
    w_router_t = jnp.concatenate([router_group_w[l].T, jnp.zeros((8 - N_GROUPS_E, D_MODEL), f32), router_expert_w[l].T], axis=0)
    b_router = jnp.concatenate([router_group_b[l], jnp.zeros((8 - N_GROUPS_E,), f32), router_expert_b[l]]).reshape(-1, 1)
    return w_main, w_small, gains, w_router_t, b_router


def kernel(x_prompt, x_sample, cache_fox_k, cache_fox_v, cache_fox_logf, cache_moba_k, cache_moba_v, state_conv, state_ssm, page_table, rel_bias, g_mix, w_in, fox_q_gain, fox_k_gain, fox_f_bias, moba_q_gain, moba_k_gain, conv_w, conv_b, dt_bias, a_log, d_skip, ssm_norm_g, w_out_fox, w_out_moba, w_out_ssm, w_o, g_ffn, router_group_w, router_group_b, router_expert_w, router_expert_b, expert_w1, expert_w3, expert_w2):
    bp, lp, _ = x_prompt.shape
    bs, ls, _ = x_sample.shape
    tp, ts = bp * lp, bs * ls
    depth = w_in.shape[0]
    x = jnp.concatenate([x_prompt.reshape(tp, D_MODEL), x_sample.reshape(ts, D_MODEL)], axis=0)

    bias_d, bias_p, bias_dt, bias_pt = bias_tiles(rel_bias)
    rows = N_HEADS * ls
    bias_own = bias_d[:, :ls, :ls].reshape(rows, ls)
    bias_last = bias_p[:, :ls, :].reshape(rows, MOBA_BLOCK)
    zero_conv = jnp.zeros((bp, SSM_CONV - 1, CONV_DIM), f32)
    zero_ssm = jnp.zeros((bp, SSM_HEADS, SSM_P, SSM_N), f32)

    new_p = [[] for _ in range(7)]
    new_s = [[] for _ in range(7)]
    for l in range(depth):
        w_main, w_small, gains, w_router_t, b_router = _prep_layer(
            l, w_in, fox_q_gain, fox_k_gain, moba_q_gain, moba_k_gain, router_group_w, router_group_b,
            router_expert_w, router_expert_b)
        proj, small, qkv16, kv_t = in_proj(x, g_mix[l], w_main, w_small, gains, tm=1024)
        logf, cum = logf_cum(small, fox_f_bias[l], lp)

        ya_p = attn_prompt("fox", proj, qkv16, bp, lp, C_FQ, C_FK, C_FV, (cum,))
        ym_p = attn_prompt("moba", proj, qkv16, bp, lp, C_MQ, C_MK, C_MV, (bias_dt, bias_pt))
        ya_s = attn_sample("fox", proj, tp, bs, ls, C_FQ, C_FK, C_FV, (cache_fox_k, cache_fox_v), page_table, l,
                           (logf, cache_fox_logf))
        ym_s = attn_sample("moba", proj, tp, bs, ls, C_MQ, C_MK, C_MV, (cache_moba_k, cache_moba_v), page_table, l,
                           (bias_own, bias_last))
        ssd_w = (conv_w[l], conv_b[l], dt_bias[l], a_log[l], d_skip[l], ssm_norm_g[l])
        yc_p, conv_p, ssm_p = ssd(proj, small, 0, bp, lp, math.gcd(lp, 128), 0, zero_conv, zero_ssm, *ssd_w)
        yc_s, conv_s, ssm_s = ssd(proj, small, tp, bs, ls, math.gcd(ls, 128), l, state_conv, state_ssm, *ssd_w)

        x = merge(x, jnp.concatenate([ya_p, ya_s]), jnp.concatenate([ym_p, ym_s]), jnp.concatenate([yc_p, yc_s]), proj,
                  w_out_fox[l].astype(bf16), w_out_moba[l].astype(bf16), w_out_ssm[l].astype(bf16), w_o[l].astype(bf16),
                  tm=256)
        x = moe(x, g_ffn[l], w_router_t, b_router, l, expert_w1, expert_w3, expert_w2,
                tm_router=512, tm_rows=256, blk=256)

        def heads(slot, r0, b, s):
            return kv_t[slot, :, r0:r0 + b * s].reshape(N_HEADS, HEAD_DIM, b, s).transpose(2, 3, 0, 1)

        for lst, r0, b, s, conv_n, ssm_n in ((new_p, 0, bp, lp, conv_p, ssm_p), (new_s, tp, bs, ls, conv_s, ssm_s)):
            lst[0].append(heads(KV_FK, r0, b, s))
            lst[1].append(heads(KV_FV, r0, b, s))
            lst[2].append(logf[r0:r0 + b * s].reshape(b, s, N_HEADS))
            lst[3].append(heads(KV_MK, r0, b, s))
            lst[4].append(heads(KV_MV, r0, b, s))
            lst[5].append(conv_n)
            lst[6].append(ssm_n)

    yp = x[:tp].reshape(bp, lp, D_MODEL)
    ys = x[tp:].reshape(bs, ls, D_MODEL)
    return (yp, ys) + tuple(jnp.stack(a) for a in new_p) + tuple(jnp.stack(a) for a in new_s)
```

```python
import functools
import math

import jax
import jax.numpy as jnp
from jax import lax
from jax.experimental import pallas as pl
from jax.experimental.pallas import tpu as pltpu

f32, bf16, i32 = jnp.float32, jnp.bfloat16, jnp.int32
HI = lax.Precision.HIGHEST
NT = (((1,), (1,)), ((), ()))
TN = (((0,), (0,)), ((), ()))

D_MODEL = 1024
HEAD_DIM = 64
N_HEADS = 8
ATT_W = N_HEADS * HEAD_DIM
ATT_SCALE = HEAD_DIM ** -0.5
PAGE = 128
MOBA_BLOCK = 256
MOBA_TOPK = 3
T5_BUCKETS = 32
T5_MAX_DIST = 128
SSM_HEADS = 16
SSM_P = 64
SSM_N = 128
SSM_GROUPS = 2
SSM_INNER = 1024
SSM_CONV = 4
CONV_DIM = 1536
N_GROUPS_E = 4
E_PER_GROUP = 8
N_EXPERTS = 32
D_EXPERT = 512
RMS_EPS = 1e-6
VMEM_LIMIT = 56 * 1024 * 1024

C_FQ, C_FK, C_MQ, C_MK, C_FV, C_MV, C_Z, C_XS, C_GL, C_BC, C_END = (
    0, 512, 1024, 1536, 2048, 2560, 3072, 4096, 5120, 8192, 8704)
N_NORMED = 4
N_ATT_TILES = 6
N_KV_SLOTS = 4
KV_FK, KV_MK, KV_FV, KV_MV = 0, 1, 2, 3
PROJ_TN = 512
S_DT, S_FF, S_W = 0, 16, 128


def _cp(sem):
    return pltpu.CompilerParams(dimension_semantics=sem, vmem_limit_bytes=VMEM_LIMIT)


def _softplus(x):
    return jnp.maximum(x, 0.0) + jnp.log1p(jnp.exp(-jnp.abs(x)))


def _silu(x):
    return x * (1.0 / (1.0 + jnp.exp(-x)))


def _bias_tiles_body(rb_ref, d_ref, p_ref, dt_ref, pt_ref):
    h = pl.program_id(0)
    n = MOBA_BLOCK
    row = lax.broadcasted_iota(i32, (n, n), 0)
    col = lax.broadcasted_iota(i32, (n, n), 1)
    max_exact = T5_BUCKETS // 2
    vals = []
    for dist, off in ((row - col, 0), (row - col, n), (col - row, 0), (col - row, n)):
        d = jnp.maximum(dist + off, 0)
        ratio = jnp.maximum(d, 1).astype(f32) / max_exact
        large = max_exact + (jnp.log(ratio) / math.log(T5_MAX_DIST / max_exact) * (T5_BUCKETS - max_exact)).astype(i32)
        bucket = jnp.where(d < max_exact, d, jnp.minimum(large, T5_BUCKETS - 1))
        val = jnp.zeros((n, n), f32)
        for k in range(T5_BUCKETS):
            val = jnp.where(bucket == k, rb_ref[k, h], val)
        vals.append(val)
    far = vals[1][n - 1:n, 0:1]
    for ref, val in zip((d_ref, p_ref, dt_ref, pt_ref), vals):
        ref[0] = val - far


def bias_tiles(rel_bias):
    n = MOBA_BLOCK
    return pl.pallas_call(
        _bias_tiles_body,
        grid=(N_HEADS,),
        in_specs=[pl.BlockSpec(memory_space=pltpu.SMEM)],
        out_specs=[pl.BlockSpec((1, n, n), lambda h: (h, 0, 0))] * 4,
        out_shape=[jax.ShapeDtypeStruct((N_HEADS, n, n), f32)] * 4,
        compiler_params=_cp(("arbitrary",)),
        name="bias_tiles",
    )(rel_bias)


def _kv_slot(j):
    return jnp.where(j < 3, 0, jnp.minimum(j - 2, N_KV_SLOTS - 1))


def _in_proj_body(x_ref, g_ref, w_ref, ws_ref, gain_ref, bd_ref, o_ref, os_ref, o16_ref, ot_ref, h_scr):
    j = pl.program_id(1)

    @pl.when(j == 0)
    def _():
        x = x_ref[...]
        h = x * lax.rsqrt(jnp.mean(x * x, axis=-1, keepdims=True) + RMS_EPS) * g_ref[...]
        hb = h.astype(bf16)
        h_scr[...] = hb
        os_ref[...] = jnp.dot(hb, ws_ref[...], preferred_element_type=f32)

    acc = jnp.dot(h_scr[...], w_ref[...], preferred_element_type=f32)

    @pl.when(j < N_NORMED)
    def _():
        sq = acc * acc
        hi = sq.astype(bf16)
        lo = (sq - hi.astype(f32)).astype(bf16)
        ms = (jnp.dot(hi, bd_ref[...], preferred_element_type=f32)
              + jnp.dot(lo, bd_ref[...], preferred_element_type=f32))
        normed = acc * lax.rsqrt(ms + RMS_EPS) * gain_ref[0]
        o_ref[...] = normed
        o16_ref[...] = normed.astype(bf16)

        @pl.when((j == 1) | (j == 3))
        def _():
            ot_ref[0] = normed.T

    @pl.when(j >= N_NORMED)
    def _():
        o_ref[...] = acc

    @pl.when((j >= N_NORMED) & (j < N_ATT_TILES))
    def _():
        o16_ref[...] = acc.astype(bf16)
        ot_ref[0] = acc.T


def in_proj(x, g_mix, w_main, w_small, gains, tm):
    t = x.shape[0]
    nj = C_END // PROJ_TN
    head_avg = jnp.kron(jnp.eye(N_HEADS, dtype=f32), jnp.full((HEAD_DIM, HEAD_DIM), 1.0 / HEAD_DIM, f32)).astype(bf16)
    return pl.pallas_call(
        _in_proj_body,
        grid=(t // tm, nj),
        in_specs=[
            pl.BlockSpec((tm, D_MODEL), lambda i, j: (i, 0)),
            pl.BlockSpec((1, D_MODEL), lambda i, j: (0, 0)),
            pl.BlockSpec((D_MODEL, PROJ_TN), lambda i, j: (0, j)),
            pl.BlockSpec((D_MODEL, S_W), lambda i, j: (0, 0)),
            pl.BlockSpec((1, 1, PROJ_TN), lambda i, j: (jnp.minimum(j, N_NORMED - 1), 0, 0)),
            pl.BlockSpec((PROJ_TN, PROJ_TN), lambda i, j: (0, 0)),
        ],
        out_specs=[
            pl.BlockSpec((tm, PROJ_TN), lambda i, j: (i, j)),
            pl.BlockSpec((tm, S_W), lambda i, j: (i, 0)),
            pl.BlockSpec((tm, PROJ_TN), lambda i, j: (i, jnp.minimum(j, N_ATT_TILES - 1))),
            pl.BlockSpec((1, PROJ_TN, tm), lambda i, j: (_kv_slot(j), 0, i)),
        ],
        out_shape=[jax.ShapeDtypeStruct((t, C_END), f32), jax.ShapeDtypeStruct((t, S_W), f32),
                   jax.ShapeDtypeStruct((t, N_ATT_TILES * PROJ_TN), bf16),
                   jax.ShapeDtypeStruct((N_KV_SLOTS, PROJ_TN, t), f32)],
        scratch_shapes=[pltpu.VMEM((tm, D_MODEL), bf16)],
        compiler_params=_cp(("arbitrary", "arbitrary")),
        name="in_proj",
    )(x, g_mix.reshape(1, D_MODEL), w_main, w_small, gains, head_avg)


def _logf_body(s_ref, b_ref, lf_ref, cum_ref, carry, *, chunks_per_seq):
    i = pl.program_id(0)
    n = s_ref.shape[0]

    @pl.when(i % chunks_per_seq == 0)
    def _():
        carry[...] = jnp.zeros_like(carry)

    ff = s_ref[:, S_FF:S_FF + N_HEADS] + b_ref[...]
    logf = -_softplus(-ff)
    lf_ref[...] = logf
    row = lax.broadcasted_iota(i32, (n, n), 0)
    col = lax.broadcasted_iota(i32, (n, n), 1)
    tri = (row >= col).astype(f32)
    c = jnp.dot(tri, logf, precision=HI, preferred_element_type=f32) + carry[...]
    cum_ref[...] = c
    carry[...] = c[n - 1:n, :]


def logf_cum(small, f_bias, seq_len, tc=256):
    t = small.shape[0]
    return pl.pallas_call(
        functools.partial(_logf_body, chunks_per_seq=seq_len // tc),
        grid=(t // tc,),
        in_specs=[pl.BlockSpec((tc, S_W), lambda i: (i, 0)), pl.BlockSpec((1, N_HEADS), lambda i: (0, 0))],
        out_specs=[pl.BlockSpec((tc, N_HEADS), lambda i: (i, 0))] * 2,
        out_shape=[jax.ShapeDtypeStruct((t, N_HEADS), f32)] * 2,
        scratch_shapes=[pltpu.VMEM((1, N_HEADS), f32)],
        compiler_params=_cp(("arbitrary",)),
        name="logf_cum",
    )(small, f_bias.reshape(1, N_HEADS))


def _half_masks():
    lane = lax.broadcasted_iota(i32, (1, 2 * HEAD_DIM), 1)
    return [lane < HEAD_DIM, lane >= HEAD_DIM]


AUG = 2 * HEAD_DIM
N_TERMS = 3


def _split_terms(x):
    hi = x.astype(bf16)
    r1 = x - hi.astype(f32)
    mid = r1.astype(bf16)
    lo = (r1 - mid.astype(f32)).astype(bf16)
    return [hi, mid, lo]


def _route_terms(x, lane_of):
    hrow = lax.broadcasted_iota(i32, (N_HEADS, AUG), 0)
    lane = lax.broadcasted_iota(i32, (N_HEADS, AUG), 1)
    out = None
    for t, term in enumerate(_split_terms(x)):
        place = (lane == lane_of(t, hrow)).astype(bf16)
        part = jnp.dot(term, place, preferred_element_type=f32)
        out = part if out is None else out + part
    return out


def _values_t(v_ref, vt_scr):
    vt = v_ref[0]
    sub = lax.broadcasted_iota(i32, (2 * HEAD_DIM, 1), 0)
    vt_scr[0] = jnp.where(sub < HEAD_DIM, vt, 1.0).astype(bf16)
    vt_scr[1] = jnp.where(sub >= HEAD_DIM, vt, 1.0).astype(bf16)


def _kt_step(s_t, m, acc, vt):
    m_new = jnp.maximum(m, jnp.max(s_t, axis=0, keepdims=True))
    alpha = jnp.exp(m - m_new)
    p_t = jnp.exp(s_t - m_new).astype(bf16)
    return m_new, acc * alpha + jnp.dot(vt, p_t, preferred_element_type=f32)


def _kt_init(tq):
    return (jnp.full((1, tq), -jnp.inf, f32), jnp.zeros((2 * HEAD_DIM, tq), f32))


def _kt_output(res):
    a0, a1 = res[0][1], res[1][1]
    sub = lax.broadcasted_iota(i32, (2 * HEAD_DIM, 1), 0)
    out_t = jnp.where(sub < HEAD_DIM, a0 / a0[HEAD_DIM:HEAD_DIM + 1, :], a1 / a1[0:1, :])
    return out_t.T


def _fox_prompt_body(q_ref, k_ref, v_ref, cq_ref, cseq_ref, o_ref, vt_scr, kaug_scr, *, tq):
    pair = pl.program_id(1)
    qi = pl.program_id(2)
    half = _half_masks()
    lane = lax.broadcasted_iota(i32, (1, AUG), 1)
    ones_at = 2 * N_TERMS

    @pl.when(qi == 0)
    def _():
        _values_t(v_ref, vt_scr)
        def lane_of(t, h):
            hh = h - 2 * pair
            return jnp.where((hh == 0) | (hh == 1), N_TERMS * hh + t, -1)

        aug = _route_terms(-cseq_ref[...], lane_of)
        aug = jnp.where((lane >= ones_at) & (lane < ones_at + N_TERMS), 1.0, aug)
        kaug_scr[:, 0:AUG] = k_ref[...]
        kaug_scr[:, AUG:2 * AUG] = aug.astype(bf16)

    q = q_ref[...] * ATT_SCALE
    cq = cq_ref[...]
    qa = []
    for hh in (0, 1):
        aug = _route_terms(cq, lambda t, h, hh=hh: jnp.where(h == 2 * pair + hh, ones_at + t, -1))
        aug = jnp.where((lane >= N_TERMS * hh) & (lane < N_TERMS * (hh + 1)), 1.0, aug)
        qa.append(jnp.concatenate([jnp.where(half[hh], q, 0.0), aug], axis=1).astype(bf16))

    def scores(kb):
        off = pl.multiple_of(kb * tq, tq)
        kt = kaug_scr[pl.ds(off, tq), :]
        return tuple(lax.dot_general(kt, qa[hh], NT, preferred_element_type=f32) for hh in (0, 1))

    def consume(kb, s_pair, carry, causal):
        off = pl.multiple_of(kb * tq, tq)
        out = []
        for hh in (0, 1):
            s_t = s_pair[hh]
            if causal is not None:
                s_t = jnp.where(causal, s_t, -jnp.inf)
            out.append(_kt_step(s_t, *carry[hh], vt_scr[hh, :, pl.ds(off, tq)]))
        return tuple(out)

    def body(kb, c):
        s_pair, carry = c
        s_next = scores(kb + 1)
        return s_next, consume(kb, s_pair, carry, None)

    s_pair, res = lax.fori_loop(0, qi, body, (scores(0), (_kt_init(tq), _kt_init(tq))))
    causal = lax.broadcasted_iota(i32, (tq, tq), 0) <= lax.broadcasted_iota(i32, (tq, tq), 1)
    res = consume(qi, s_pair, res, causal)
    o_ref[...] = _kt_output(res)


def _rank_select(gates):
    out = []
    for n, gn in enumerate(gates):
        rank = jnp.zeros(gn.shape, i32)
        for m, gm in enumerate(gates):
            if m < n:
                rank = rank + (gm >= gn).astype(i32)
            elif m > n:
                rank = rank + (gm > gn).astype(i32)
        out.append(rank < MOBA_TOPK)
    return out


def _moba_prompt_body(q_ref, k_ref, v_ref, bdt_ref, bpt_ref, k32_ref, o_ref, vt_scr, kmean_scr, *, tq, nblk):
    qi = pl.program_id(2)
    half = _half_masks()
    neg = jnp.float32(-jnp.inf)

    @pl.when(qi == 0)
    def _():
        _values_t(v_ref, vt_scr)
        kmean_scr[...] = jnp.mean(k32_ref[...].reshape(nblk, tq, 2 * HEAD_DIM), axis=1)

    q = q_ref[...]
    qs = q * ATT_SCALE
    qh = [jnp.where(half[hh], qs, 0.0).astype(bf16) for hh in (0, 1)]
    kmean = kmean_scr[...]
    idx = lax.broadcasted_iota(i32, (1, nblk), 1)
    sel_t = []
    for hh in (0, 1):
        gate = lax.dot_general(jnp.where(half[hh], q, 0.0).astype(bf16), kmean.astype(bf16), NT, preferred_element_type=f32)
        g = jnp.where(idx < qi, gate, neg)
        rank = jnp.zeros(g.shape, i32)
        for m in range(nblk):
            gm = g[:, m:m + 1]
            rank = rank + ((gm > g) | ((gm == g) & (m < idx))).astype(i32)
        mask = jnp.where((rank < MOBA_TOPK) & (idx < qi), 0.0, neg)
        sel_t.append(mask.T)

    def mask_row(kb, hh):
        mk = sel_t[hh][0:1, :]
        for n in range(1, nblk):
            mk = jnp.where(kb == n, sel_t[hh][n:n + 1, :], mk)
        return mk

    def scores(kb):
        off = pl.multiple_of(kb * tq, tq)
        kt = k_ref[pl.ds(off, tq), :]
        return tuple(lax.dot_general(kt, qh[hh], NT, preferred_element_type=f32) for hh in (0, 1))

    def consume(kb, s_pair, carry, bias_ref, causal, masked):
        off = pl.multiple_of(kb * tq, tq)
        out = []
        for hh in (0, 1):
            s_t = s_pair[hh]
            if bias_ref is not None:
                s_t = s_t + bias_ref[hh]
            if causal is not None:
                s_t = jnp.where(causal, s_t, neg)
            if masked:
                s_t = s_t + mask_row(kb, hh)
            out.append(_kt_step(s_t, *carry[hh], vt_scr[hh, :, pl.ds(off, tq)]))
        return tuple(out)

    causal = lax.broadcasted_iota(i32, (tq, tq), 0) <= lax.broadcasted_iota(i32, (tq, tq), 1)
    prev = jnp.maximum(qi - 1, 0)
    s_prev = scores(prev)
    res = consume(qi, scores(qi), (_kt_init(tq), _kt_init(tq)), bdt_ref, causal, False)
    s_far = scores(0)
    res = consume(prev, s_prev, res, bpt_ref, None, True)

    def body(kb, c):
        s_pair, carry = c
        s_next = scores(jnp.minimum(kb + 1, nblk - 1))
        return s_next, consume(kb, s_pair, carry, None, None, True)

    _, res = lax.fori_loop(0, qi - 1, body, (s_far, res))
    o_ref[...] = _kt_output(res)


def attn_prompt(mode, proj, qkv16, kv_t, batch, seq, c_q, c_k, slot_v, extra, tq=MOBA_BLOCK):
    nq = seq // tq
    npair = N_HEADS // 2
    pw = 2 * HEAD_DIM
    in_specs = [
        pl.BlockSpec((tq, pw), lambda b, p, i: (b * nq + i, c_q // pw + p)),
        pl.BlockSpec((seq, pw), lambda b, p, i: (b, c_k // pw + p)),
        pl.BlockSpec((1, pw, seq), lambda b, p, i: (slot_v, p, b)),
    ]
    scratch = [pltpu.VMEM((2, pw, seq), bf16)]
    if mode == "fox":
        (cum,) = extra
        body = functools.partial(_fox_prompt_body, tq=tq)
        in_specs += [pl.BlockSpec((tq, N_HEADS), lambda b, p, i: (b * nq + i, 0)),
                     pl.BlockSpec((seq, N_HEADS), lambda b, p, i: (b, 0))]
        args = (cum, cum)
        scratch += [pltpu.VMEM((seq, pw + AUG), bf16)]
    else:
        bias_dt, bias_pt = extra
        body = functools.partial(_moba_prompt_body, tq=tq, nblk=nq)
        in_specs += [pl.BlockSpec((2, tq, tq), lambda b, p, i: (p, 0, 0))] * 2 + [in_specs[1]]
        args = (bias_dt, bias_pt, proj)
        scratch += [pltpu.VMEM((nq, pw), f32)]
    return pl.pallas_call(
        body,
        grid=(batch, npair, nq),
        in_specs=in_specs,
        out_specs=pl.BlockSpec((tq, pw), lambda b, p, i: (b * nq + i, p)),
        out_shape=jax.ShapeDtypeStruct((batch * seq, ATT_W), f32),
        scratch_shapes=scratch,
        compiler_params=_cp(("arbitrary", "arbitrary", "arbitrary")),
        name=f"{mode}_prompt",
    )(proj, qkv16, kv_t, *args)


def _expand_q(q):
    nq = q.shape[0]
    rows = N_HEADS * nq
    qt = jnp.broadcast_to(q[None], (N_HEADS, nq, ATT_W)).reshape(rows, ATT_W)
    rh = lax.broadcasted_iota(i32, (rows, ATT_W), 0) // nq
    lh = lax.broadcasted_iota(i32, (rows, ATT_W), 1) // HEAD_DIM
    return qt, rh == lh


def _rows_from_heads(x_t, nq):
    return jnp.broadcast_to(x_t[:, None, :], (N_HEADS, nq, x_t.shape[1])).reshape(N_HEADS * nq, x_t.shape[1])


def _cum_lanes(lf, carry):
    n = lf.shape[1]
    upper = (lax.broadcasted_iota(i32, (n, n), 0) <= lax.broadcasted_iota(i32, (n, n), 1)).astype(bf16)
    hi = lf.astype(bf16).astype(f32)
    mid = (lf - hi).astype(bf16).astype(f32)
    lo = lf - hi - mid
    parts = jnp.dot(jnp.concatenate([hi, mid, lo], axis=0).astype(bf16), upper, preferred_element_type=f32)
    h = lf.shape[0]
    c = parts[0:h] + parts[h:2 * h] + parts[2 * h:3 * h] + carry
    return c, c[:, n - 1:n]


def _sample_attn_body(pt_ref, q_ref, kn_ref, vn_ref, *refs, mode, npages, nq):
    del pt_ref
    if mode == "fox":
        lfn_ref, refs = refs[0], refs[1:]
    else:
        bo_ref, bl_ref, refs = refs[0], refs[1], refs[2:]
    kp, vp, refs = refs[:npages], refs[npages:2 * npages], refs[2 * npages:]
    if mode == "fox":
        lp, refs = refs[:npages], refs[npages:]
    o_ref = refs[0]
    rows = N_HEADS * nq
    neg = jnp.float32(-jnp.inf)

    q = q_ref[...]
    qt, diag = _expand_q(q)
    qe = jnp.where(diag, qt * ATT_SCALE, 0.0).astype(bf16)
    s_past = [jnp.dot(qe, kp[j][0].astype(bf16), preferred_element_type=f32) for j in range(npages)]
    s_new = lax.dot_general(qe, kn_ref[...].astype(bf16), NT, preferred_element_type=f32)
    qpos = lax.broadcasted_iota(i32, (rows, nq), 0) % nq
    kpos = lax.broadcasted_iota(i32, (rows, nq), 1)

    if mode == "fox":
        carry = jnp.zeros((N_HEADS, 1), f32)
        for j in range(npages):
            c, carry = _cum_lanes(lp[j][0], carry)
            s_past[j] = s_past[j] - _rows_from_heads(c, nq)
        c, _ = _cum_lanes(lfn_ref[...].T, carry)
        s_new = s_new - _rows_from_heads(c, nq)
    else:
        ppb = MOBA_BLOCK // PAGE
        nblk = npages // ppb
        q_t = q.astype(bf16).astype(f32).T
        gates = []
        for n in range(nblk):
            ksum = kp[ppb * n][0]
            for j in range(1, ppb):
                ksum = ksum + kp[ppb * n + j][0]
            kmean = jnp.sum(ksum, axis=1, keepdims=True) * (1.0 / MOBA_BLOCK)
            kmean = kmean.astype(bf16).astype(f32)
            gates.append(jnp.sum((q_t * kmean).reshape(N_HEADS, HEAD_DIM, nq), axis=1))
        picks = _rank_select(gates)
        own_lane = kpos == qpos
        for n in range(nblk):
            spread = _rows_from_heads(jnp.where(picks[n], 0.0, neg), nq)
            mask = jnp.min(jnp.where(own_lane, spread, 0.0), axis=1, keepdims=True)
            for j in range(ppb):
                pg = ppb * n + j
                s = s_past[pg] + mask
                if n == nblk - 1:
                    s = s + bl_ref[:, j * PAGE:(j + 1) * PAGE]
                s_past[pg] = s
        s_new = s_new + bo_ref[...]

    s_new = jnp.where(kpos <= qpos, s_new, neg)
    m = jnp.max(s_new, axis=-1, keepdims=True)
    for j in range(npages):
        m = jnp.maximum(m, jnp.max(s_past[j], axis=-1, keepdims=True))
    p_new = jnp.exp(s_new - m)
    l = jnp.sum(p_new, axis=-1, keepdims=True)
    p_past = []
    for j in range(npages):
        p_past.append(jnp.exp(s_past[j] - m))
        l = l + jnp.sum(p_past[j], axis=-1, keepdims=True)
    inv = 1.0 / l
    acc = jnp.dot((p_new * inv).astype(bf16), vn_ref[...].astype(bf16), preferred_element_type=f32)
    for j in range(npages):
        acc = acc + lax.dot_general((p_past[j] * inv).astype(bf16), vp[j][0].astype(bf16), NT, preferred_element_type=f32)
    out = jnp.where(diag, acc, 0.0)
    o_ref[...] = jnp.sum(out.reshape(N_HEADS, nq, ATT_W), axis=0)


def attn_sample(mode, proj, row0, n_seq, nq, c_q, c_k, c_v, pools, page_table, layer, extra):
    k_pool, v_pool = pools
    n_pool = k_pool.shape[1]
    npages = page_table.shape[1]
    assert (npages * PAGE) % MOBA_BLOCK == 0 and nq <= MOBA_BLOCK
    k_t = k_pool.transpose(0, 1, 3, 4, 2).reshape(-1, ATT_W, PAGE)
    v_t = v_pool.transpose(0, 1, 3, 4, 2).reshape(-1, ATT_W, PAGE)
    rb0 = row0 // nq

    def new_spec(c0):
        return pl.BlockSpec((nq, ATT_W), lambda b, pt: (rb0 + b, c0 // ATT_W))

    def page_spec(j, height):
        return pl.BlockSpec((1, height, PAGE), lambda b, pt: (layer * n_pool + pt[b, j], 0, 0))

    rows = N_HEADS * nq
    kv_specs = [page_spec(j, ATT_W) for j in range(npages)] * 2
    if mode == "fox":
        logf_new, lf_pool = extra
        lf_t = lf_pool.transpose(0, 1, 3, 2).reshape(-1, N_HEADS, PAGE)
        in_specs = ([new_spec(c_q), new_spec(c_k), new_spec(c_v), pl.BlockSpec((nq, N_HEADS), lambda b, pt: (rb0 + b, 0))]
                    + kv_specs + [page_spec(j, N_HEADS) for j in range(npages)])
        args = (proj, proj, proj, logf_new) + (k_t,) * npages + (v_t,) * npages + (lf_t,) * npages
    else:
        bias_own, bias_last = extra
        in_specs = ([new_spec(c_q), new_spec(c_k), new_spec(c_v),
                     pl.BlockSpec((rows, nq), lambda b, pt: (0, 0)), pl.BlockSpec((rows, MOBA_BLOCK), lambda b, pt: (0, 0))]
                    + kv_specs)
        args = (proj, proj, proj, bias_own, bias_last) + (k_t,) * npages + (v_t,) * npages
    return pl.pallas_call(
        functools.partial(_sample_attn_body, mode=mode, npages=npages, nq=nq),
        grid_spec=pltpu.PrefetchScalarGridSpec(
            num_scalar_prefetch=1,
            grid=(n_seq,),
            in_specs=in_specs,
            out_specs=pl.BlockSpec((nq, ATT_W), lambda b, pt: (b, 0)),
        ),
        out_shape=jax.ShapeDtypeStruct((n_seq * nq, ATT_W), f32),
        compiler_params=_cp(("arbitrary",)),
        name=f"{mode}_sample",
    )(page_table, *args)


def _ssd_body(z_ref, xs_ref, bc_ref, sm_ref, cs_ref, h0_ref, cw_ref, cb_ref, dtb_ref, alog_ref, dsk_ref, ng_ref,
              y_ref, conv_ref, ssm_ref, xp_scr, ht_scr, yd_scr, *, cl):
    c = pl.program_id(1)
    nc = pl.num_programs(1)
    tail = SSM_CONV - 1
    base = 8
    nh, hp, ns = SSM_HEADS, SSM_P, SSM_N
    gw = SSM_INNER // SSM_GROUPS

    @pl.when(c == 0)
    def _():
        xp_scr[base - tail:base, :] = cs_ref[0]
        ht_scr[...] = h0_ref[0].T

    xp_scr[base:base + cl, 0:SSM_INNER] = xs_ref[...]
    xp_scr[base:base + cl, SSM_INNER:CONV_DIM] = bc_ref[...]
    conv = cb_ref[...]
    for w in range(SSM_CONV):
        conv = conv + xp_scr[pl.ds(base - tail + w, cl), :] * cw_ref[w:w + 1, :]
    new_tail = xp_scr[base + cl - tail:base + cl, :]
    xp_scr[base - tail:base, :] = new_tail
    u = _silu(conv)
    xs = u[:, :SSM_INNER]
    bm = [u[:, SSM_INNER + g * ns:SSM_INNER + (g + 1) * ns] for g in range(SSM_GROUPS)]
    cm = [u[:, SSM_INNER + (SSM_GROUPS + g) * ns:SSM_INNER + (SSM_GROUPS + g + 1) * ns] for g in range(SSM_GROUPS)]

    dt = _softplus(sm_ref[:, S_DT:S_DT + nh] + dtb_ref[...])
    a = dt * (-jnp.exp(alog_ref[...]))
    row = lax.broadcasted_iota(i32, (cl, cl), 0)
    col = lax.broadcasted_iota(i32, (cl, cl), 1)
    causal = row >= col
    acum = jnp.dot(causal.astype(f32), a, precision=HI, preferred_element_type=f32)
    eye = (lax.broadcasted_iota(i32, (nh, nh), 0) == lax.broadcasted_iota(i32, (nh, nh), 1)).astype(f32)
    acum_t = lax.dot_general(eye, acum, NT, precision=HI, preferred_element_type=f32)
    a_end = acum[cl - 1:cl, :]
    spread = (lax.broadcasted_iota(i32, (nh, SSM_INNER), 1) // hp == lax.broadcasted_iota(i32, (nh, SSM_INNER), 0)).astype(f32)

    def lanes(x):
        return jnp.dot(x, spread, precision=HI, preferred_element_type=f32)

    xdt = xs * lanes(dt)
    half = _half_masks()

    cb = [lax.dot_general(cm[g].astype(bf16), bm[g].astype(bf16), NT, preferred_element_type=f32) for g in range(SSM_GROUPS)]
    for pr in range(nh // 2):
        xp_pair = xdt[:, pr * 2 * hp:(pr + 1) * 2 * hp]
        acc = None
        for hh in (0, 1):
            hd = 2 * pr + hh
            seg = acum[:, hd:hd + 1] - acum_t[hd:hd + 1, :]
            decay = jnp.exp(jnp.where(causal, seg, -jnp.inf))
            mm = (cb[hd // (nh // SSM_GROUPS)] * decay).astype(bf16)
            t = jnp.dot(mm, jnp.where(half[hh], xp_pair, 0.0).astype(bf16), preferred_element_type=f32)
            acc = t if acc is None else acc + t
        yd_scr[:, pr * 2 * hp:(pr + 1) * 2 * hp] = acc

    ht = ht_scr[...]
    xw = (xdt * lanes(jnp.exp(a_end - acum))).astype(bf16)
    y_off = []
    st = []
    for g in range(SSM_GROUPS):
        y_off.append(jnp.dot(cm[g].astype(bf16), ht[:, g * gw:(g + 1) * gw].astype(bf16), preferred_element_type=f32))
        st.append(lax.dot_general(bm[g].astype(bf16), xw[:, g * gw:(g + 1) * gw], TN, preferred_element_type=f32))
    from_start = lanes(jnp.exp(acum))
    y = yd_scr[...] + jnp.concatenate(y_off, axis=1) * from_start + xs * dsk_ref[...]
    ht_new = ht * from_start[cl - 1:cl, :] + jnp.concatenate(st, axis=1)
    ht_scr[...] = ht_new

    yg = y * _silu(z_ref[...])
    outs = []
    for g in range(SSM_GROUPS):
        part = yg[:, g * gw:(g + 1) * gw]
        outs.append(part * lax.rsqrt(jnp.mean(part * part, axis=-1, keepdims=True) + RMS_EPS))
    y_ref[...] = jnp.concatenate(outs, axis=1) * ng_ref[...]

    @pl.when(c == nc - 1)
    def _():
        conv_ref[0] = new_tail
        ssm_ref[0] = ht_new.T


def ssd(proj, small, row0, n_seq, seq, cl, layer, conv_state, ssm_state, conv_w, conv_b, dt_bias, a_log, d_skip, norm_g):
    nc = seq // cl
    rb0 = row0 // cl
    s0 = layer * n_seq
    conv_state = conv_state.reshape(-1, SSM_CONV - 1, CONV_DIM)
    h0 = ssm_state.reshape(-1, SSM_HEADS * SSM_P, SSM_N)

    def rows(width, c0):
        return pl.BlockSpec((cl, width), lambda b, c: (rb0 + b * nc + c, c0 // width))

    def const(shape):
        return pl.BlockSpec(shape, lambda b, c: (0,) * len(shape))

    y, new_conv, new_ssm = pl.pallas_call(
        functools.partial(_ssd_body, cl=cl),
        grid=(n_seq, nc),
        in_specs=[
            rows(SSM_INNER, C_Z), rows(SSM_INNER, C_XS), rows(CONV_DIM - SSM_INNER, C_BC), rows(S_W, 0),
            pl.BlockSpec((1, SSM_CONV - 1, CONV_DIM), lambda b, c: (s0 + b, 0, 0)),
            pl.BlockSpec((1, SSM_HEADS * SSM_P, SSM_N), lambda b, c: (s0 + b, 0, 0)),
            const((SSM_CONV, CONV_DIM)), const((1, CONV_DIM)), const((1, SSM_HEADS)), const((1, SSM_HEADS)),
            const((1, SSM_INNER)), const((1, SSM_INNER)),
        ],
        out_specs=[
            pl.BlockSpec((cl, SSM_INNER), lambda b, c: (b * nc + c, 0)),
            pl.BlockSpec((1, SSM_CONV - 1, CONV_DIM), lambda b, c: (b, 0, 0)),
            pl.BlockSpec((1, SSM_HEADS * SSM_P, SSM_N), lambda b, c: (b, 0, 0)),
        ],
        out_shape=[
            jax.ShapeDtypeStruct((n_seq * seq, SSM_INNER), f32),
            jax.ShapeDtypeStruct((n_seq, SSM_CONV - 1, CONV_DIM), f32),
            jax.ShapeDtypeStruct((n_seq, SSM_HEADS * SSM_P, SSM_N), f32),
        ],
        scratch_shapes=[pltpu.VMEM((8 + cl, CONV_DIM), f32), pltpu.VMEM((SSM_N, SSM_INNER), f32),
                        pltpu.VMEM((cl, SSM_INNER), f32)],
        compiler_params=_cp(("arbitrary", "arbitrary")),
        name=f"ssd_cl{cl}",
    )(proj, proj, proj, small, conv_state, h0, conv_w, conv_b.reshape(1, CONV_DIM), dt_bias.reshape(1, SSM_HEADS),
      a_log.reshape(1, SSM_HEADS), jnp.repeat(d_skip, SSM_P).reshape(1, SSM_INNER), norm_g.reshape(1, SSM_INNER))
    return y, new_conv, new_ssm.reshape(n_seq, SSM_HEADS, SSM_P, SSM_N)


def _merge_body(x_ref, ya_ref, ym_ref, yc_ref, ga_ref, gb_ref, gc_ref, wa_ref, wb_ref, wc_ref, wo_ref, o_ref):
    def branch(y_ref, w_ref, g_ref):
        return jax.nn.sigmoid(g_ref[...]) * jnp.dot(y_ref[...].astype(bf16), w_ref[...], preferred_element_type=f32)

    merged = branch(ya_ref, wa_ref, ga_ref) + branch(ym_ref, wb_ref, gb_ref) + branch(yc_ref, wc_ref, gc_ref)
    o_ref[...] = x_ref[...] + jnp.dot(merged.astype(bf16), wo_ref[...], preferred_element_type=f32)


def merge(x, ya, ym, yc, proj, wa, wb, wc, wo, tm):
    t = x.shape[0]
    g0 = C_GL // D_MODEL

    def rows(width, cb=0):
        return pl.BlockSpec((tm, width), lambda i: (i, cb))

    def const(shape):
        return pl.BlockSpec(shape, lambda i: (0, 0))

    return pl.pallas_call(
        _merge_body,
        grid=(t // tm,),
        in_specs=[rows(D_MODEL), rows(ATT_W), rows(ATT_W), rows(SSM_INNER),
                  rows(D_MODEL, g0), rows(D_MODEL, g0 + 1), rows(D_MODEL, g0 + 2),
                  const((ATT_W, D_MODEL)), const((ATT_W, D_MODEL)), const((SSM_INNER, D_MODEL)), const((D_MODEL, D_MODEL))],
        out_specs=rows(D_MODEL),
        out_shape=jax.ShapeDtypeStruct((t, D_MODEL), f32),
        compiler_params=_cp(("arbitrary",)),
        name="merge",
    )(x, ya, ym, yc, proj, proj, proj, wa, wb, wc, wo)


def _router_body(x_ref, g_ref, wr_ref, br_ref, h_ref, eid_ref, rank_ref, ewt_ref, cnt_ref, cnt_scr, *, tm):
    i = pl.program_id(0)

    @pl.when(i == 0)
    def _():
        cnt_scr[...] = jnp.zeros_like(cnt_scr)

    x = x_ref[...]
    h = x * lax.rsqrt(jnp.mean(x * x, axis=-1, keepdims=True) + RMS_EPS) * g_ref[...]
    h_ref[...] = h
    logits = lax.dot_general(wr_ref[...].astype(bf16), h.astype(bf16), NT, preferred_element_type=f32) + br_ref[...]
    sub = lax.broadcasted_iota(i32, (8, tm), 0)
    gl = jnp.where(sub < N_GROUPS_E, logits[0:8], -jnp.inf)
    gmax = jnp.max(gl, axis=0, keepdims=True)
    g_top = 1.0 / jnp.sum(jnp.exp(gl - gmax), axis=0, keepdims=True)
    g_idx = jnp.min(jnp.where(gl == gmax, sub, 8), axis=0, keepdims=True)
    e_in = jnp.zeros((E_PER_GROUP, tm), f32)
    for g in range(N_GROUPS_E):
        e_in = jnp.where(g_idx == g, logits[8 + g * E_PER_GROUP:8 + (g + 1) * E_PER_GROUP], e_in)
    ex = jnp.exp(e_in - jnp.max(e_in, axis=0, keepdims=True))
    prob = ex / jnp.sum(ex, axis=0, keepdims=True)
    p1 = jnp.max(prob, axis=0, keepdims=True)
    i1 = jnp.min(jnp.where(prob == p1, sub, 8), axis=0, keepdims=True)
    rest = jnp.where(sub == i1, -1.0, prob)
    p2 = jnp.max(rest, axis=0, keepdims=True)
    i2 = jnp.min(jnp.where(rest == p2, sub, 8), axis=0, keepdims=True)
    denom = p1 + p2
    ids = [g_idx * E_PER_GROUP + i1, g_idx * E_PER_GROUP + i2]
    wts = [g_top * p1 / denom, g_top * p2 / denom]
    eid_ref[...] = jnp.concatenate(ids, axis=0)
    ewt_ref[...] = jnp.concatenate(wts + [jnp.zeros((6, tm), f32)], axis=0).T

    esub = lax.broadcasted_iota(i32, (N_EXPERTS, tm), 0)
    oh = [(esub == ids[k]).astype(f32) for k in (0, 1)]
    both = oh[0] + oh[1]
    before = (lax.broadcasted_iota(i32, (tm, tm), 0) < lax.broadcasted_iota(i32, (tm, tm), 1)).astype(bf16)
    pos = jnp.dot(both.astype(bf16), before, preferred_element_type=f32) + cnt_scr[:, 0:1]
    rank_ref[...] = jnp.concatenate([jnp.sum(oh[k] * pos, axis=0, keepdims=True) for k in (0, 1)], axis=0).astype(i32)
    cnt_scr[...] = cnt_scr[...] + jnp.sum(both, axis=1, keepdims=True)
    cnt_ref[...] = cnt_scr[...].astype(i32)


def router(x, g_ffn, w_router_t, b_router, tm):
    t = x.shape[0]
    nr = w_router_t.shape[0]
    return pl.pallas_call(
        functools.partial(_router_body, tm=tm),
        grid=(t // tm,),
        in_specs=[pl.BlockSpec((tm, D_MODEL), lambda i: (i, 0)), pl.BlockSpec((1, D_MODEL), lambda i: (0, 0)),
                  pl.BlockSpec((nr, D_MODEL), lambda i: (0, 0)), pl.BlockSpec((nr, 1), lambda i: (0, 0))],
        out_specs=[pl.BlockSpec((tm, D_MODEL), lambda i: (i, 0)),
                   pl.BlockSpec((2, tm), lambda i: (0, i)), pl.BlockSpec((2, tm), lambda i: (0, i)),
                   pl.BlockSpec((tm, 8), lambda i: (i, 0)),
                   pl.BlockSpec((N_EXPERTS, 128), lambda i: (0, 0))],
        out_shape=[jax.ShapeDtypeStruct((t, D_MODEL), f32),
                   jax.ShapeDtypeStruct((2, t), i32), jax.ShapeDtypeStruct((2, t), i32),
                   jax.ShapeDtypeStruct((t, 8), f32),
                   jax.ShapeDtypeStruct((N_EXPERTS, 128), i32)],
        scratch_shapes=[pltpu.VMEM((N_EXPERTS, 128), f32)],
        compiler_params=_cp(("arbitrary",)),
        name="router",
    )(x, g_ffn.reshape(1, D_MODEL), w_router_t, b_router)


def _dispatch_body(pstart_ref, eid_ref, rank_ref, h_ref, xs_in_ref, xs_ref, dest_ref, sem, *, tm):
    del xs_in_ref

    def copy(t, d):
        return pltpu.make_async_copy(h_ref.at[pl.ds(t, 1), :], xs_ref.at[pl.ds(d, 1), :], sem)

    def issue(t, _):
        for k in (0, 1):
            d = pstart_ref[eid_ref[k, t]] + rank_ref[k, t]
            dest_ref[k, t] = d
            copy(t, d).start()
        return 0

    lax.fori_loop(0, tm, issue, 0)

    def drain(t, _):
        for k in (0, 1):
            copy(t, dest_ref[k, t]).wait()
        return 0

    lax.fori_loop(0, tm, drain, 0)


def dispatch(h, eid, rank, pstart, cap, tm):
    t = h.shape[0]
    smem_rows = pl.BlockSpec((2, tm), lambda i, ps: (0, i), memory_space=pltpu.SMEM)
    return pl.pallas_call(
        functools.partial(_dispatch_body, tm=tm),
        grid_spec=pltpu.PrefetchScalarGridSpec(
            num_scalar_prefetch=1,
            grid=(t // tm,),
            in_specs=[smem_rows, smem_rows, pl.BlockSpec((tm, D_MODEL), lambda i, ps: (i, 0)),
                      pl.BlockSpec(memory_space=pl.ANY)],
            out_specs=[pl.BlockSpec(memory_space=pl.ANY), smem_rows],
            scratch_shapes=[pltpu.SemaphoreType.DMA(())],
        ),
        out_shape=[jax.ShapeDtypeStruct((cap, D_MODEL), f32), jax.ShapeDtypeStruct((2, t), i32)],
        input_output_aliases={4: 0},
        compiler_params=_cp(("arbitrary",)),
        name="moe_dispatch",
    )(pstart, eid, rank, h, jnp.zeros((cap, D_MODEL), f32))


def _experts_body(be_ref, nu_ref, x_ref, w1_ref, w3_ref, w2_ref, o_ref):
    @pl.when(pl.program_id(0) < nu_ref[0])
    def _():
        xb = x_ref[...].astype(bf16)
        a = jnp.dot(xb, w1_ref[0].astype(bf16), preferred_element_type=f32)
        b = jnp.dot(xb, w3_ref[0].astype(bf16), preferred_element_type=f32)
        o_ref[...] = jnp.dot((_silu(a) * b).astype(bf16), w2_ref[0].astype(bf16), preferred_element_type=f32)

    @pl.when(pl.program_id(0) >= nu_ref[0])
    def _():
        o_ref[...] = jnp.zeros_like(o_ref)


def experts(xs, blk_expert, n_used, w1, w3, w2, blk):
    cap = xs.shape[0]

    def row(i, be, nu):
        return jnp.minimum(i, nu[0] - 1)

    return pl.pallas_call(
        _experts_body,
        grid_spec=pltpu.PrefetchScalarGridSpec(
            num_scalar_prefetch=2,
            grid=(cap // blk,),
            in_specs=[pl.BlockSpec((blk, D_MODEL), lambda i, be, nu: (row(i, be, nu), 0)),
                      pl.BlockSpec((1, D_MODEL, D_EXPERT), lambda i, be, nu: (be[row(i, be, nu)], 0, 0)),
                      pl.BlockSpec((1, D_MODEL, D_EXPERT), lambda i, be, nu: (be[row(i, be, nu)], 0, 0)),
                      pl.BlockSpec((1, D_EXPERT, D_MODEL), lambda i, be, nu: (be[row(i, be, nu)], 0, 0))],
            out_specs=pl.BlockSpec((blk, D_MODEL), lambda i, be, nu: (i, 0)),
        ),
        out_shape=jax.ShapeDtypeStruct((cap, D_MODEL), f32),
        compiler_params=_cp(("arbitrary",)),
        name="moe_experts",
    )(blk_expert, n_used, xs, w1, w3, w2)


def _combine_body(dest_ref, x_ref, ewt_ref, ys_ref, o_ref, buf, sem, *, tm):
    def copy(t, k):
        return pltpu.make_async_copy(ys_ref.at[pl.ds(dest_ref[k, t], 1), :], buf.at[k, pl.ds(t, 1), :], sem)

    def issue(t, _):
        for k in (0, 1):
            copy(t, k).start()
        return 0

    lax.fori_loop(0, tm, issue, 0)

    def drain(t, _):
        for k in (0, 1):
            copy(t, k).wait()
        return 0

    lax.fori_loop(0, tm, drain, 0)
    w = ewt_ref[...]
    o_ref[...] = x_ref[...] + (buf[0] * w[:, 0:1] + buf[1] * w[:, 1:2])


def combine(x, ewt, dest, ys, tm):
    t = x.shape[0]
    return pl.pallas_call(
        functools.partial(_combine_body, tm=tm),
        grid=(t // tm,),
        in_specs=[pl.BlockSpec((2, tm), lambda i: (0, i), memory_space=pltpu.SMEM),
                  pl.BlockSpec((tm, D_MODEL), lambda i: (i, 0)), pl.BlockSpec((tm, 8), lambda i: (i, 0)),
                  pl.BlockSpec(memory_space=pl.ANY)],
        out_specs=pl.BlockSpec((tm, D_MODEL), lambda i: (i, 0)),
        out_shape=jax.ShapeDtypeStruct((t, D_MODEL), f32),
        scratch_shapes=[pltpu.VMEM((2, tm, D_MODEL), f32), pltpu.SemaphoreType.DMA(())],
        compiler_params=_cp(("arbitrary",)),
        name="moe_combine",
    )(dest, x, ewt, ys)


def moe(x, g_ffn, w_router_t, b_router, layer, w1, w3, w2, tm_router, tm_rows, blk):
    t = x.shape[0]
    h, eid, rank, ewt, counts = router(x, g_ffn, w_router_t, b_router, tm_router)
    counts = counts[:, 0]
    padded = (counts + blk - 1) // blk * blk
    pend = jnp.cumsum(padded)
    pstart = (pend - padded).astype(i32)
    nblocks = -(-2 * t // blk) + N_EXPERTS
    blk_start = jnp.arange(nblocks, dtype=i32) * blk
    blk_expert = jnp.minimum(jnp.sum((pend[None, :] <= blk_start[:, None]).astype(i32), axis=1), N_EXPERTS - 1)
    xs, dest = dispatch(h, eid, rank, pstart, nblocks * blk, tm_rows)
    flat = lambda w: w.reshape((-1,) + w.shape[2:])
    n_used = (pend[N_EXPERTS - 1:] // blk).astype(i32)
    ys = experts(xs, blk_expert + layer * N_EXPERTS, n_used, flat(w1), flat(w3), flat(w2), blk)
    return combine(x, ewt, dest, ys, tm_rows)


def _prep_layer(l, w_in, fox_q_gain, fox_k_gain, moba_q_gain, moba_k_gain, router_group_w, router_group_b,
                router_expert_w, router_expert_b):
    w = w_in[l]
    o = [0]
    for s in (ATT_W, ATT_W, ATT_W, N_HEADS, ATT_W, ATT_W, ATT_W, SSM_INNER, CONV_DIM, SSM_HEADS, 3 * D_MODEL):
        o.append(o[-1] + s)
    fq, fk, fv, ff, mq, mk, mv, z, xbc, dtr, gl = [w[:, o[i]:o[i + 1]] for i in range(11)]
    w_main = jnp.concatenate([fq, fk, mq, mk, fv, mv, z, xbc[:, :SSM_INNER], gl, xbc[:, SSM_INNER:]], axis=1).astype(bf16)
    w_small = jnp.concatenate([dtr, ff, jnp.zeros((D_MODEL, S_W - SSM_HEADS - N_HEADS), f32)], axis=1).astype(bf16)
    gains = jnp.stack([jnp.tile(g[l], N_HEADS) for g in (fox_q_gain, fox_k_gain, moba_q_gain, moba_k_gain)]).reshape(4, 1, ATT_W)
    w_router_t = jnp.concatenate([router_group_w[l].T, jnp.zeros((8 - N_GROUPS_E, D_MODEL), f32), router_expert_w[l].T], axis=0)
    b_router = jnp.concatenate([router_group_b[l], jnp.zeros((8 - N_GROUPS_E,), f32), router_expert_b[l]]).reshape(-1, 1)
    return w_main, w_small, gains, w_router_t, b_router


def kernel(x_prompt, x_sample, cache_fox_k, cache_fox_v, cache_fox_logf, cache_moba_k, cache_moba_v, state_conv, state_ssm, page_table, rel_bias, g_mix, w_in, fox_q_gain, fox_k_gain, fox_f_bias, moba_q_gain, moba_k_gain, conv_w, conv_b, dt_bias, a_log, d_skip, ssm_norm_g, w_out_fox, w_out_moba, w_out_ssm, w_o, g_ffn, router_group_w, router_group_b, router_expert_w, router_expert_b, expert_w1, expert_w3, expert_w2):
    bp, lp, _ = x_prompt.shape
    bs, ls, _ = x_sample.shape
    tp, ts = bp * lp, bs * ls
    depth = w_in.shape[0]
    x = jnp.concatenate([x_prompt.reshape(tp, D_MODEL), x_sample.reshape(ts, D_MODEL)], axis=0)

    bias_d, bias_p, bias_dt, bias_pt = bias_tiles(rel_bias)
    rows = N_HEADS * ls
    bias_own = bias_d[:, :ls, :ls].reshape(rows, ls)
    bias_last = bias_p[:, :ls, :].reshape(rows, MOBA_BLOCK)
    zero_conv = jnp.zeros((bp, SSM_CONV - 1, CONV_DIM), f32)
    zero_ssm = jnp.zeros((bp, SSM_HEADS, SSM_P, SSM_N), f32)

    new_p = [[] for _ in range(7)]
    new_s = [[] for _ in range(7)]
    for l in range(depth):
        w_main, w_small, gains, w_router_t, b_router = _prep_layer(
            l, w_in, fox_q_gain, fox_k_gain, moba_q_gain, moba_k_gain, router_group_w, router_group_b,
            router_expert_w, router_expert_b)
        proj, small, qkv16, kv_t = in_proj(x, g_mix[l], w_main, w_small, gains, tm=1024)
        logf, cum = logf_cum(small, fox_f_bias[l], lp)

        ya_p = attn_prompt("fox", proj, qkv16, kv_t, bp, lp, C_FQ, C_FK, KV_FV, (cum,))
        ym_p = attn_prompt("moba", proj, qkv16, kv_t, bp, lp, C_MQ, C_MK, KV_MV, (bias_dt, bias_pt))
        ya_s = attn_sample("fox", proj, tp, bs, ls, C_FQ, C_FK, C_FV, (cache_fox_k, cache_fox_v), page_table, l,
                           (logf, cache_fox_logf))
        ym_s = attn_sample("moba", proj, tp, bs, ls, C_MQ, C_MK, C_MV, (cache_moba_k, cache_moba_v), page_table, l,
                           (bias_own, bias_last))
        ssd_w = (conv_w[l], conv_b[l], dt_bias[l], a_log[l], d_skip[l], ssm_norm_g[l])
        yc_p, conv_p, ssm_p = ssd(proj, small, 0, bp, lp, math.gcd(lp, 128), 0, zero_conv, zero_ssm, *ssd_w)
        yc_s, conv_s, ssm_s = ssd(proj, small, tp, bs, ls, math.gcd(ls, 128), l, state_conv, state_ssm, *ssd_w)

        x = merge(x, jnp.concatenate([ya_p, ya_s]), jnp.concatenate([ym_p, ym_s]), jnp.concatenate([yc_p, yc_s]), proj,
                  w_out_fox[l].astype(bf16), w_out_moba[l].astype(bf16), w_out_ssm[l].astype(bf16), w_o[l].astype(bf16),
                  tm=256)
        x = moe(x, g_ffn[l], w_router_t, b_router, l, expert_w1, expert_w3, expert_w2,
                tm_router=512, tm_rows=256, blk=256)

        def heads(slot, r0, b, s):
            return kv_t[slot, :, r0:r0 + b * s].reshape(N_HEADS, HEAD_DIM, b, s).transpose(2, 3, 0, 1)

        for lst, r0, b, s, conv_n, ssm_n in ((new_p, 0, bp, lp, conv_p, ssm_p), (new_s, tp, bs, ls, conv_s, ssm_s)):
            lst[0].append(heads(KV_FK, r0, b, s))
            lst[1].append(heads(KV_FV, r0, b, s))
            lst[2].append(logf[r0:r0 + b * s].reshape(b, s, N_HEADS))
            lst[3].append(heads(KV_MK, r0, b, s))
            lst[4].append(heads(KV_MV, r0, b, s))
            lst[5].append(conv_n)
            lst[6].append(ssm_n)

    yp = x[:tp].reshape(bp, lp, D_MODEL)
    ys = x[tp:].reshape(bs, ls, D_MODEL)
    return (yp, ys) + tuple(jnp.stack(a) for a in new_p) + tuple(jnp.stack(a) for a in new_s)
```
